```python
import math
import jax, jax.numpy as jnp
from jax import lax
import numpy as np

D_MODEL = 1024
BATCH = 8
SEQ = 2048
DEPTH = 1
DEC_BATCH = 128
DEC_SEQ = 8
PAST_LEN = 16384
PAGE_SIZE = 128

D_LRU = D_MODEL // 2
LRU_HEADS = 8
LRU_HD = D_LRU // LRU_HEADS
LRU_CONV = 4
LRU_C = 8.0
D_SC = D_MODEL // 4
SC_CONV = 3
MEM_HEADS = 4
MEM_HD = (D_MODEL // 4) // MEM_HEADS
D_MEM = MEM_HEADS * MEM_HD
N_MEM = 256
D_MIX = D_LRU + D_SC + D_MEM
D_IN = 2 * D_LRU + 3 * D_SC + D_MEM
PEER_HEADS = 8
N_KEYS = 128
N_EXPERTS = N_KEYS * N_KEYS
PEER_DK = 256
PEER_TOPK = 16
PEER_BLOCK = 128
LN_EPS = 1e-5
RMS_EPS = 1e-6
DEEPNORM_ALPHA = (2.0 * DEPTH) ** 0.25
DEEPNORM_BETA = (8.0 * DEPTH) ** -0.25

kernel_name = 'hymba_rglru_shortconv_peer_step'


def _layernorm(x, g, b):
    xf = x.astype(jnp.float32)
    mu = jnp.mean(xf, axis=-1, keepdims=True)
    var = jnp.mean(jnp.square(xf - mu), axis=-1, keepdims=True)
    y = (xf - mu) * lax.rsqrt(var + LN_EPS) * g.astype(jnp.float32) + b.astype(jnp.float32)
    return y.astype(x.dtype)


def _rmsnorm(x, g):
    xf = x.astype(jnp.float32)
    y = xf * lax.rsqrt(jnp.mean(jnp.square(xf), axis=-1, keepdims=True) + RMS_EPS) * g.astype(jnp.float32)
    return y.astype(x.dtype)


def _causal_dwconv(x, buf, w):
    width = w.shape[0]
    s = x.shape[1]
    full = jnp.concatenate([buf.astype(x.dtype), x], axis=1)
    y = full[:, 0:s] * w[0]
    for k in range(1, width):
        y = y + full[:, k:k + s] * w[k]
    return y, full[:, full.shape[1] - (width - 1):]


def _linear_scan(a, b, h0):
    b = b.at[:, 0].add(a[:, 0] * h0)

    def comb(l, r):
        return (l[0] * r[0], r[0] * l[1] + r[1])

    _, h = lax.associative_scan(comb, (a, b), axis=1)
    return h


def _mem_kv(mem, w_mem_kv):
    bsz = mem.shape[0]
    kv = jnp.einsum('bmd,de->bme', mem, w_mem_kv)
    k, v = jnp.split(kv, 2, axis=-1)
    return (k.reshape(bsz, N_MEM, MEM_HEADS, MEM_HD), v.reshape(bsz, N_MEM, MEM_HEADS, MEM_HD))


def _mixer(x, mem_k, mem_v, lru_buf, h0, sc_buf, lp):
    bsz, s, _ = x.shape
    proj = jnp.einsum('bsd,de->bse', x, lp['w_in'])
    cuts = [D_LRU, 2 * D_LRU, 2 * D_LRU + D_SC, 2 * D_LRU + 2 * D_SC, 2 * D_LRU + 3 * D_SC]
    xl, gate, sc_b, sc_c, sc_x, q = jnp.split(proj, cuts, axis=-1)
    xc, new_lru_buf = _causal_dwconv(xl, lru_buf, lp['lru_conv_w'])
    xc = xc + lp['lru_conv_b']
    xh = xc.reshape(bsz, s, LRU_HEADS, LRU_HD)
    r = jax.nn.sigmoid(jnp.einsum('bshi,hij->bshj', xh, lp['w_rg_a']).reshape(bsz, s, D_LRU) + lp['b_rg_a'])
    i = jax.nn.sigmoid(jnp.einsum('bshi,hij->bshj', xh, lp['w_rg_x']).reshape(bsz, s, D_LRU) + lp['b_rg_x'])
    log_a = LRU_C * r.astype(jnp.float32) * jax.nn.log_sigmoid(lp['lru_lambda'].astype(jnp.float32))
    a = jnp.exp(log_a)
    bt = jnp.sqrt(-jnp.expm1(2.0 * log_a)) * (i * xc).astype(jnp.float32)
    h = _linear_scan(a, bt, h0.astype(jnp.float32))
    y_lru = h.astype(x.dtype) * jax.nn.gelu(gate, approximate=False)
    u, new_sc_buf = _causal_dwconv(sc_c * sc_x, sc_buf, lp['sc_conv_w'])
    y_sc = sc_b * u
    qh = q.reshape(bsz, s, MEM_HEADS, MEM_HD)
    sc = jnp.einsum('bshd,bmhd->bhsm', qh, mem_k).astype(jnp.float32) * (MEM_HD ** -0.5)
    p = jax.nn.softmax(sc, axis=-1).astype(x.dtype)
    y_mem = jnp.einsum('bhsm,bmhd->bshd', p, mem_v).reshape(bsz, s, D_MEM)
    y = jnp.concatenate([_rmsnorm(y_lru, lp['g_lru']), _rmsnorm(y_sc, lp['g_sc']), _rmsnorm(y_mem, lp['g_mem'])], axis=-1)
    out = jnp.einsum('bse,ed->bsd', y, lp['w_out'])
    return out, new_lru_buf, h[:, -1], new_sc_buf


def _peer(x, wq, keys, u_tab, v_tab):
    bsz, s, d = x.shape
    t = bsz * s
    n_blk = -(-t // PEER_BLOCK)
    xt = jnp.pad(x.reshape(t, d), ((0, n_blk * PEER_BLOCK - t), (0, 0))).reshape(n_blk, PEER_BLOCK, d)

    def block(xb):
        qb = jnp.einsum('td,de->te', xb, wq).reshape(PEER_BLOCK, PEER_HEADS, 2, PEER_DK // 2)
        sk = jnp.einsum('thpd,hpkd->thpk', qb, keys).astype(jnp.float32)
        v1, i1 = lax.top_k(sk[:, :, 0], PEER_TOPK)
        v2, i2 = lax.top_k(sk[:, :, 1], PEER_TOPK)
        cand = (v1[..., :, None] + v2[..., None, :]).reshape(PEER_BLOCK, PEER_HEADS, PEER_TOPK * PEER_TOPK)
        top_s, ci = lax.top_k(cand, PEER_TOPK)
        e = (jnp.take_along_axis(i1, ci // PEER_TOPK, axis=-1) * N_KEYS
             + jnp.take_along_axis(i2, ci % PEER_TOPK, axis=-1))
        g = jax.nn.softmax(top_s, axis=-1).astype(xb.dtype)
        act = jax.nn.gelu(jnp.einsum('td,thkd->thk', xb, u_tab[e]), approximate=False)
        return jnp.einsum('thk,thkd->td', g * act, v_tab[e])

    out = lax.map(block, xt).reshape(n_blk * PEER_BLOCK, d)[:t]
    return out.reshape(bsz, s, d)


def _layer(x, mem_k, mem_v, lru_buf, h0, sc_buf, lp):
    m, nb, nh, ns = _mixer(x, mem_k, mem_v, lru_buf, h0, sc_buf, lp)
    x = _layernorm(DEEPNORM_ALPHA * x + m, lp['ln1_g'], lp['ln1_b'])
    f = _peer(x, lp['peer_wq'], lp['peer_keys'], lp['peer_u'], lp['peer_v'])
    x = _layernorm(DEEPNORM_ALPHA * x + f, lp['ln2_g'], lp['ln2_b'])
    return x, nb, nh, ns


def setup_inputs(seed: int = 0) -> dict:
    key = jax.random.key(seed)
    ks = jax.random.split(key, 40)
    f32 = jnp.float32
    nrm = lambda k, shape, sc: jax.random.normal(k, shape, f32) * sc
    L = DEPTH
    u_lam = jax.random.uniform(ks[10], (L, D_LRU), f32, 0.9, 0.999)
    s_lam = u_lam ** (1.0 / LRU_C)
    lam = jnp.log(s_lam) - jnp.log1p(-s_lam)
    w_k = nrm(ks[13], (L, D_MODEL, D_MEM), D_MODEL ** -0.5)
    w_v = nrm(ks[14], (L, D_MODEL, D_MEM), DEEPNORM_BETA * D_MODEL ** -0.5)
    return {
        'x_prompt': nrm(ks[0], (BATCH, SEQ, D_MODEL), 1.0),
        'x_sample': nrm(ks[1], (DEC_BATCH, DEC_SEQ, D_MODEL), 1.0),
        'mem_prompt': nrm(ks[2], (BATCH, N_MEM, D_MODEL), 1.0),
        'cache_mem_k': nrm(ks[3], (L, DEC_BATCH, N_MEM, MEM_HEADS, MEM_HD), 1.0),
        'cache_mem_v': nrm(ks[4], (L, DEC_BATCH, N_MEM, MEM_HEADS, MEM_HD), DEEPNORM_BETA),
        'state_lru_conv': nrm(ks[5], (L, DEC_BATCH, LRU_CONV - 1, D_LRU), 1.0),
        'state_lru_h': nrm(ks[6], (L, DEC_BATCH, D_LRU), 0.5),
        'state_sc_conv': nrm(ks[7], (L, DEC_BATCH, SC_CONV - 1, D_SC), 1.0),
        'w_in': nrm(ks[8], (L, D_MODEL, D_IN), D_MODEL ** -0.5),
        'lru_conv_w': nrm(ks[9], (L, LRU_CONV, D_LRU), LRU_CONV ** -0.5),
        'lru_conv_b': nrm(ks[11], (L, D_LRU), 0.01),
        'w_rg_a': nrm(ks[12], (L, LRU_HEADS, LRU_HD, LRU_HD), LRU_HD ** -0.5),
        'b_rg_a': nrm(ks[15], (L, D_LRU), 0.01),
        'w_rg_x': nrm(ks[16], (L, LRU_HEADS, LRU_HD, LRU_HD), LRU_HD ** -0.5),
        'b_rg_x': nrm(ks[17], (L, D_LRU), 0.01),
        'lru_lambda': lam,
        'sc_conv_w': nrm(ks[18], (L, SC_CONV, D_SC), SC_CONV ** -0.5),
        'w_mem_kv': jnp.concatenate([w_k, w_v], axis=-1),
        'g_lru': 1.0 + nrm(ks[19], (L, D_LRU), 0.02),
        'g_sc': 1.0 + nrm(ks[20], (L, D_SC), 0.02),
        'g_mem': 1.0 + nrm(ks[21], (L, D_MEM), 0.02),
        'w_out': nrm(ks[22], (L, D_MIX, D_MODEL), DEEPNORM_BETA * D_MIX ** -0.5),
        'ln1_g': 1.0 + nrm(ks[23], (L, D_MODEL), 0.02),
        'ln1_b': nrm(ks[24], (L, D_MODEL), 0.02),
        'peer_wq': nrm(ks[25], (L, D_MODEL, PEER_HEADS * PEER_DK), D_MODEL ** -0.5),
        'peer_keys': nrm(ks[26], (L, PEER_HEADS, 2, N_KEYS, PEER_DK // 2), (PEER_DK // 2) ** -0.5),
        'peer_u': nrm(ks[27], (L, N_EXPERTS, D_MODEL), D_MODEL ** -0.5),
        'peer_v': nrm(ks[28], (L, N_EXPERTS, D_MODEL), DEEPNORM_BETA),
        'ln2_g': 1.0 + nrm(ks[29], (L, D_MODEL), 0.02),
        'ln2_b': nrm(ks[30], (L, D_MODEL), 0.02),
    }


def reference(x_prompt, x_sample, mem_prompt, cache_mem_k, cache_mem_v, state_lru_conv, state_lru_h,
              state_sc_conv, w_in, lru_conv_w, lru_conv_b, w_rg_a, b_rg_a, w_rg_x, b_rg_x, lru_lambda,
              sc_conv_w, w_mem_kv, g_lru, g_sc, g_mem, w_out, ln1_g, ln1_b, peer_wq, peer_keys, peer_u,
              peer_v, ln2_g, ln2_b):
    xp, xs = x_prompt, x_sample
    bsz = x_prompt.shape[0]
    mk_p, mv_p, lc_p, lh_p, sc_p, lc_s, lh_s, sc_s = [], [], [], [], [], [], [], []
    for l in range(DEPTH):
        lp = {'w_in': w_in[l], 'lru_conv_w': lru_conv_w[l], 'lru_conv_b': lru_conv_b[l],
              'w_rg_a': w_rg_a[l], 'b_rg_a': b_rg_a[l], 'w_rg_x': w_rg_x[l], 'b_rg_x': b_rg_x[l],
              'lru_lambda': lru_lambda[l], 'sc_conv_w': sc_conv_w[l], 'g_lru': g_lru[l], 'g_sc': g_sc[l],
              'g_mem': g_mem[l], 'w_out': w_out[l], 'ln1_g': ln1_g[l], 'ln1_b': ln1_b[l],
              'peer_wq': peer_wq[l], 'peer_keys': peer_keys[l], 'peer_u': peer_u[l], 'peer_v': peer_v[l],
              'ln2_g': ln2_g[l], 'ln2_b': ln2_b[l]}
        mk, mv = _mem_kv(mem_prompt, w_mem_kv[l])
        xp, nb, nh, ns = _layer(xp, mk, mv,
                                jnp.zeros((bsz, LRU_CONV - 1, D_LRU), xp.dtype),
                                jnp.zeros((bsz, D_LRU), jnp.float32),
                                jnp.zeros((bsz, SC_CONV - 1, D_SC), xp.dtype), lp)
        mk_p.append(mk); mv_p.append(mv); lc_p.append(nb); lh_p.append(nh); sc_p.append(ns)
        xs, nb, nh, ns = _layer(xs, cache_mem_k[l], cache_mem_v[l], state_lru_conv[l], state_lru_h[l],
                                state_sc_conv[l], lp)
        lc_s.append(nb); lh_s.append(nh); sc_s.append(ns)
    mem_k_prompt = jnp.stack(mk_p)
    mem_v_prompt = jnp.stack(mv_p)
    lru_conv_prompt = jnp.stack(lc_p)
    lru_h_prompt = jnp.stack(lh_p)
    sc_conv_prompt = jnp.stack(sc_p)
    lru_conv_sample = jnp.stack(lc_s)
    lru_h_sample = jnp.stack(lh_s)
    sc_conv_sample = jnp.stack(sc_s)
    return (xp, xs, mem_k_prompt, mem_v_prompt, lru_conv_prompt, lru_h_prompt, sc_conv_prompt,
            lru_conv_sample, lru_h_sample, sc_conv_sample)
```

```python
import functools
import math

import jax
import jax.numpy as jnp
from jax import lax
from jax.experimental import pallas as pl
from jax.experimental.pallas import tpu as pltpu

F32 = jnp.float32
BF16 = jnp.bfloat16

LRU_C = 8.0
LN_EPS = 1e-5
RMS_EPS = 1e-6
PEER_TOPK = 16

V7X_LANES = 128
V7X_SUBLANES = 8
V7X_VMEM_LIMIT_BYTES = 56 * 1024 * 1024

_NT = (((1,), (1,)), ((), ()))


def _gelu_exact(x):
    return 0.5 * x * (1.0 + lax.erf(x * (1.0 / math.sqrt(2.0))))


def _layernorm(x, g, b):
    mu = jnp.mean(x, axis=-1, keepdims=True)
    xc = x - mu
    var = jnp.mean(xc * xc, axis=-1, keepdims=True)
    return xc * lax.rsqrt(var + LN_EPS) * g + b


def _rmsnorm(x, g):
    return x * lax.rsqrt(jnp.mean(x * x, axis=-1, keepdims=True) + RMS_EPS) * g


def _memkv_kernel(mem_ref, w_ref, k_ref, v_ref):
    d_mem = k_ref.shape[-1]
    kv = jnp.dot(mem_ref[0].astype(BF16), w_ref[...], preferred_element_type=F32)
    k_ref[0] = kv[:, :d_mem]
    v_ref[0] = kv[:, d_mem:]


def _memkv(mem, w_kv_bf16):
    bsz, n_mem, d_model = mem.shape
    d_mem = w_kv_bf16.shape[1] // 2
    out = jax.ShapeDtypeStruct((bsz, n_mem, d_mem), F32)
    return pl.pallas_call(
        _memkv_kernel,
        grid=(bsz,),
        in_specs=[
            pl.BlockSpec((1, n_mem, d_model), lambda b: (b, 0, 0)),
            pl.BlockSpec((d_model, 2 * d_mem), lambda b: (0, 0)),
        ],
        out_specs=[
            pl.BlockSpec((1, n_mem, d_mem), lambda b: (b, 0, 0)),
            pl.BlockSpec((1, n_mem, d_mem), lambda b: (b, 0, 0)),
        ],
        out_shape=[out, out],
        name="memkv",
    )(mem, w_kv_bf16)


_CONV_PAD = V7X_SUBLANES


def _causal_conv(buf_ref, x3, w_ref, width):
    nb, ts, c = x3.shape
    buf_ref[:, _CONV_PAD:_CONV_PAD + ts, :] = x3
    acc = x3 * w_ref[width - 1:width, :].reshape(1, 1, c)
    for k in range(width - 1):
        start = _CONV_PAD - (width - 1) + k
        acc = acc + buf_ref[:, start:start + ts, :] * w_ref[k:k + 1, :].reshape(1, 1, c)
    tail = buf_ref[:, _CONV_PAD + ts - (width - 1):_CONV_PAD + ts, :]
    buf_ref[:, _CONV_PAD - (width - 1):_CONV_PAD, :] = tail
    return acc, tail


def _mixer_kernel(alpha, n_heads_mem,
                  x_ref, mk_ref, mv_ref, lc0_ref, h0_ref, sc0_ref,
                  w_in_ref, cw_ref, cb_ref, wa_ref, ba_ref, wx_ref, bx_ref, lam_ref,
                  scw_ref, g_lru_ref, g_sc_ref, g_mem_ref, w_out_ref, ln_g_ref, ln_b_ref,
                  y_ref, lc_ref, h_ref, sc_ref,
                  cbuf_a, cbuf_b, a_scr, b_scr, hall, hcar, q_scr, ymem):
    nb, ts, d_model = x_ref.shape
    rows = nb * ts
    d_lru = h_ref.shape[-1]
    d_sc = sc_ref.shape[-1]
    d_mem = mk_ref.shape[-1]
    lru_w = cw_ref.shape[0]
    sc_w = scw_ref.shape[0]
    j = pl.program_id(1)

    @pl.when(j == 0)
    def _():
        cbuf_a[:, _CONV_PAD - (lru_w - 1):_CONV_PAD, :] = lc0_ref[...]
        cbuf_b[:, _CONV_PAD - (sc_w - 1):_CONV_PAD, :] = sc0_ref[...]
        hcar[...] = h0_ref[...]

    x2 = x_ref[...].reshape(rows, d_model)
    proj = jnp.dot(x2.astype(BF16), w_in_ref[...], preferred_element_type=F32)
    o = 0
    xl = proj[:, o:o + d_lru]; o += d_lru
    gate = proj[:, o:o + d_lru]; o += d_lru
    sc_b = proj[:, o:o + d_sc]; o += d_sc
    sc_c = proj[:, o:o + d_sc]; o += d_sc
    sc_x = proj[:, o:o + d_sc]; o += d_sc
    q_scr[...] = proj[:, o:o + d_mem]

    xc3, lc_tail = _causal_conv(cbuf_a, xl.reshape(nb, ts, d_lru), cw_ref, lru_w)
    lc_ref[...] = lc_tail
    xc = xc3.reshape(rows, d_lru) + cb_ref[...]
    xcb = xc.astype(BF16)
    r = jax.nn.sigmoid(jnp.dot(xcb, wa_ref[...], preferred_element_type=F32) + ba_ref[...])
    i = jax.nn.sigmoid(jnp.dot(xcb, wx_ref[...], preferred_element_type=F32) + bx_ref[...])
    lam = lam_ref[...]
    log_sig = -(jnp.maximum(-lam, 0.0) + jnp.log1p(jnp.exp(-jnp.abs(lam))))
    log_a = LRU_C * r * log_sig
    a = jnp.exp(log_a)
    bt = jnp.sqrt(-jnp.tanh(log_a) * (a * a + 1.0)) * (i * xc)
    n_lt = d_lru // V7X_LANES
    for c in range(n_lt):
        a_scr[c] = a[:, c * V7X_LANES:(c + 1) * V7X_LANES]
        b_scr[c] = bt[:, c * V7X_LANES:(c + 1) * V7X_LANES]

    def scan_step(s, hs):
        rows_s = pl.ds(s, nb, stride=ts)
        new = []
        for c in range(n_lt):
            h = a_scr[c, rows_s, :] * hs[c] + b_scr[c, rows_s, :]
            hall[c, rows_s, :] = h
            new.append(h)
        return tuple(new)

    h0 = hcar[...]
    hs = lax.fori_loop(0, ts, scan_step,
                       tuple(h0[:, c * V7X_LANES:(c + 1) * V7X_LANES] for c in range(n_lt)))
    h_last = jnp.concatenate(hs, axis=-1)
    hcar[...] = h_last
    h_ref[...] = h_last
    y_lru = jnp.concatenate([hall[c] for c in range(n_lt)], axis=-1) * _gelu_exact(gate)

    u3, sc_tail = _causal_conv(cbuf_b, (sc_c * sc_x).reshape(nb, ts, d_sc), scw_ref, sc_w)
    sc_ref[...] = sc_tail
    y_sc = sc_b * u3.reshape(rows, d_sc)

    hd = d_mem // n_heads_mem
    head_of_lane = lax.broadcasted_iota(jnp.int32, (ts, d_mem), 1) >> int(math.log2(hd))
    scale = hd ** -0.5

    def attn_step(b, carry):
        rws = pl.ds(pl.multiple_of(b * ts, ts), ts)
        qb = q_scr[rws, :]
        qm = jnp.concatenate(
            [jnp.where(head_of_lane == h, qb, 0.0) for h in range(n_heads_mem)], axis=0)
        kb = mk_ref[b].astype(BF16)
        vb = mv_ref[b].astype(BF16)
        s = lax.dot_general(qm.astype(BF16), kb, _NT, preferred_element_type=F32) * scale
        s = s - jnp.max(s, axis=-1, keepdims=True)
        p = jnp.exp(s)
        p = p / jnp.sum(p, axis=-1, keepdims=True)
        yh = jnp.dot(p.astype(BF16), vb, preferred_element_type=F32)
        yb = jnp.zeros((ts, d_mem), F32)
        for h in range(n_heads_mem):
            yb = yb + jnp.where(head_of_lane == h, yh[h * ts:(h + 1) * ts, :], 0.0)
        ymem[rws, :] = yb
        return carry

    lax.fori_loop(0, nb, attn_step, 0)

    y = jnp.concatenate(
        [_rmsnorm(y_lru, g_lru_ref[...]), _rmsnorm(y_sc, g_sc_ref[...]),
         _rmsnorm(ymem[...], g_mem_ref[...])], axis=-1)
    out = jnp.dot(y.astype(BF16), w_out_ref[...], preferred_element_type=F32)
    y_ref[...] = _layernorm(alpha * x2 + out, ln_g_ref[...], ln_b_ref[...]).reshape(nb, ts, d_model)


def _const_spec(shape):
    nd = len(shape)
    return pl.BlockSpec(shape, lambda i, j: (0,) * nd)


def _mixer(x, mem_k, mem_v, lc0, h0, sc0, wts, *, alpha, n_heads_mem, nb, ts):
    bsz, seq, d_model = x.shape
    n_mem, d_mem = mem_k.shape[1:]
    d_lru = h0.shape[-1]
    d_sc = sc0.shape[-1]
    lru_w = lc0.shape[1] + 1
    sc_w = sc0.shape[1] + 1
    rows = nb * ts
    grid = (bsz // nb, seq // ts)
    weights = [wts[k] for k in ("w_in", "lru_conv_w", "lru_conv_b", "wa", "ba", "wx", "bx", "lam",
                                "sc_conv_w", "g_lru", "g_sc", "g_mem", "w_out", "ln_g", "ln_b")]
    in_specs = [
        pl.BlockSpec((nb, ts, d_model), lambda i, j: (i, j, 0)),
        pl.BlockSpec((nb, n_mem, d_mem), lambda i, j: (i, 0, 0)),
        pl.BlockSpec((nb, n_mem, d_mem), lambda i, j: (i, 0, 0)),
        pl.BlockSpec((nb, lru_w - 1, d_lru), lambda i, j: (i, 0, 0)),
        pl.BlockSpec((nb, d_lru), lambda i, j: (i, 0)),
        pl.BlockSpec((nb, sc_w - 1, d_sc), lambda i, j: (i, 0, 0)),
    ] + [_const_spec(w.shape) for w in weights]
    out_specs = [
        pl.BlockSpec((nb, ts, d_model), lambda i, j: (i, j, 0)),
        pl.BlockSpec((nb, lru_w - 1, d_lru), lambda i, j: (i, 0, 0)),
        pl.BlockSpec((nb, d_lru), lambda i, j: (i, 0)),
        pl.BlockSpec((nb, sc_w - 1, d_sc), lambda i, j: (i, 0, 0)),
    ]
    out_shape = [
        jax.ShapeDtypeStruct((bsz, seq, d_model), F32),
        jax.ShapeDtypeStruct((bsz, lru_w - 1, d_lru), F32),
        jax.ShapeDtypeStruct((bsz, d_lru), F32),
        jax.ShapeDtypeStruct((bsz, sc_w - 1, d_sc), F32),
    ]
    scratch = [
        pltpu.VMEM((nb, ts + _CONV_PAD, d_lru), F32),
        pltpu.VMEM((nb, ts + _CONV_PAD, d_sc), F32),
        pltpu.VMEM((d_lru // V7X_LANES, rows, V7X_LANES), F32),
        pltpu.VMEM((d_lru // V7X_LANES, rows, V7X_LANES), F32),
        pltpu.VMEM((d_lru // V7X_LANES, rows, V7X_LANES), F32),
        pltpu.VMEM((nb, d_lru), F32),
        pltpu.VMEM((rows, d_mem), F32),
        pltpu.VMEM((rows, d_mem), F32),
    ]
    return pl.pallas_call(
        functools.partial(_mixer_kernel, alpha, n_heads_mem),
        grid=grid,
        in_specs=in_specs,
        out_specs=out_specs,
        out_shape=out_shape,
        scratch_shapes=scratch,
        compiler_params=pltpu.CompilerParams(
            dimension_semantics=("arbitrary", "arbitrary"),
            vmem_limit_bytes=V7X_VMEM_LIMIT_BYTES),
        name="mixer",
    )(x, mem_k, mem_v, lc0, h0, sc0, *weights)


def _top16_rows(vals, keys, big_key):
    out_v, out_k = [], []
    for _ in range(PEER_TOPK):
        m = jnp.max(vals, axis=0, keepdims=True)
        kk = jnp.min(jnp.where(vals == m, keys, big_key), axis=0, keepdims=True)
        vals = jnp.where(keys == kk, -jnp.inf, vals)
        out_v.append(m)
        out_k.append(kk)
    return out_v, out_k


def _candidate_blocks():
    k = PEER_TOPK
    blocks = []
    for a in range(V7X_SUBLANES):
        nvalid = k // (a + 1)
        for b0 in range(0, nvalid, V7X_SUBLANES):
            blocks.append(("row", a, b0, min(V7X_SUBLANES, nvalid - b0)))
    for a0 in range(V7X_SUBLANES, k, V7X_SUBLANES):
        blocks.append(("col", a0, 0, V7X_SUBLANES))
    return blocks


def _topk_kernel(n_keys, x_ref, wq_ref, keys_ref, g_ref, c_ref, j_ref,
                 v_scr, i_scr, s_scr, e_scr):
    tm = x_ref.shape[0]
    n_heads = wq_ref.shape[0]
    lanes = V7X_LANES
    k = PEER_TOPK
    xb = x_ref[...].astype(BF16)
    key_iota = lax.broadcasted_iota(jnp.int32, (n_keys, lanes), 0)
    sub_iota = lax.broadcasted_iota(jnp.int32, (V7X_SUBLANES, lanes), 0)
    n_experts_pad = n_keys * n_keys

    def head_body(h, carry):
        qh = jnp.dot(xb, wq_ref[h], preferred_element_type=F32)
        dk2 = qh.shape[1] // 2
        for p in range(2):
            qhp = qh[:, p * dk2:(p + 1) * dk2].astype(BF16)
            s_t = lax.dot_general(keys_ref[h, p], qhp, _NT, preferred_element_type=F32)
            for half in range(tm // lanes):
                vs, ks = _top16_rows(s_t[:, half * lanes:(half + 1) * lanes], key_iota, n_keys)
                for r in range(k):
                    v_scr[p, half, r:r + 1, :] = vs[r]
                    i_scr[p, half, r:r + 1, :] = ks[r]
        for half in range(tm // lanes):
            v1 = v_scr[0, half]
            v2 = v_scr[1, half]
            i1 = i_scr[0, half]
            i2 = i_scr[1, half]
            cand, ckey = [], []
            for kind, a0, b0, nvalid in _candidate_blocks():
                if kind == "row":
                    val = v1[a0:a0 + 1, :] + v2[b0:b0 + V7X_SUBLANES, :]
                    e = i1[a0:a0 + 1, :] * n_keys + i2[b0:b0 + V7X_SUBLANES, :]
                    ci = a0 * k + b0 + sub_iota
                    if nvalid < V7X_SUBLANES:
                        val = jnp.where(sub_iota < nvalid, val, -jnp.inf)
                else:
                    val = v1[a0:a0 + V7X_SUBLANES, :] + v2[0:1, :]
                    e = i1[a0:a0 + V7X_SUBLANES, :] * n_keys + i2[0:1, :]
                    ci = (a0 + sub_iota) * k
                cand.append(val)
                ckey.append(ci * n_experts_pad + e)
            ts_rows, tk_rows = _top16_rows(jnp.concatenate(cand, axis=0),
                                           jnp.concatenate(ckey, axis=0),
                                           k * k * n_experts_pad)
            for r in range(k):
                s_scr[h, r:r + 1, half * lanes:(half + 1) * lanes] = ts_rows[r]
                e_scr[h, r:r + 1, half * lanes:(half + 1) * lanes] = tk_rows[r] & (n_experts_pad - 1)
        return carry

    lax.fori_loop(0, n_heads, head_body, 0)

    s_all = s_scr[...]
    ex = jnp.exp(s_all - s_all[:, 0:1, :])
    gates = ex / jnp.sum(ex, axis=1, keepdims=True)
    e_all = e_scr[...]
    g_ref[...] = gates.reshape(n_heads * k, tm).T
    c_ref[...] = (e_all >> int(math.log2(n_keys))).astype(F32).reshape(n_heads * k, tm).T
    j_ref[...] = (e_all & (n_keys - 1)).astype(F32).reshape(n_heads * k, tm).T


def _peer_topk(x1, wq_heads, keys_bf16, *, tm):
    t, d_model = x1.shape
    n_heads, _, dk = wq_heads.shape
    n_keys = keys_bf16.shape[2]
    hk = n_heads * PEER_TOPK
    halves = tm // V7X_LANES
    out = jax.ShapeDtypeStruct((t, hk), F32)
    return pl.pallas_call(
        functools.partial(_topk_kernel, n_keys),
        grid=(t // tm,),
        in_specs=[
            pl.BlockSpec((tm, d_model), lambda i: (i, 0)),
            pl.BlockSpec(wq_heads.shape, lambda i: (0, 0, 0)),
            pl.BlockSpec(keys_bf16.shape, lambda i: (0, 0, 0, 0)),
        ],
        out_specs=[pl.BlockSpec((tm, hk), lambda i: (i, 0))] * 3,
        out_shape=[out, out, out],
        scratch_shapes=[
            pltpu.VMEM((2, halves, PEER_TOPK, V7X_LANES), F32),
            pltpu.VMEM((2, halves, PEER_TOPK, V7X_LANES), jnp.int32),
            pltpu.VMEM((n_heads, PEER_TOPK, tm), F32),
            pltpu.VMEM((n_heads, PEER_TOPK, tm), jnp.int32),
        ],
        compiler_params=pltpu.CompilerParams(
            dimension_semantics=("arbitrary",),
            vmem_limit_bytes=V7X_VMEM_LIMIT_BYTES),
        name="peer_topk",
    )(x1, wq_heads, keys_bf16)


_GATE_GROUP = 2 * V7X_SUBLANES


def _experts_kernel(alpha, n_keys, x_ref, g_ref, c_ref, j_ref, ut_ref, v_ref, ln_g_ref, ln_b_ref,
                    o_ref, gate_scr, acc, xb_scr, tmp):
    tm, d_model = x_ref.shape
    ec = ut_ref.shape[1]
    chunks_c = ec // n_keys
    kstep = pl.program_id(1)

    @pl.when(kstep == 0)
    def _():
        xb_scr[...] = x_ref[...].astype(BF16)
        acc[...] = jnp.zeros_like(acc)
        row_iota = lax.broadcasted_iota(jnp.int32, (n_keys, g_ref.shape[1]), 0).astype(F32)

        def group_body(tg, carry):
            base = pl.multiple_of(tg * _GATE_GROUP, _GATE_GROUP)
            for tl in range(_GATE_GROUP):
                row = pl.ds(base + tl, 1)
                ct = jnp.where(c_ref[row, :] == row_iota, g_ref[row, :], 0.0).astype(BF16)
                jt = jnp.where(j_ref[row, :] == row_iota, 1.0, 0.0).astype(BF16)
                tmp[tl * n_keys:(tl + 1) * n_keys, :] = lax.dot_general(
                    ct, jt, _NT, preferred_element_type=F32)
            for c in range(n_keys):
                blk = tmp[pl.ds(c, _GATE_GROUP, stride=n_keys), :]
                lane0 = (c % chunks_c) * n_keys
                gate_scr[c // chunks_c, pl.ds(base, _GATE_GROUP), lane0:lane0 + n_keys] = blk.astype(BF16)
            return carry

        lax.fori_loop(0, tm // _GATE_GROUP, group_body, 0)

    a = jnp.dot(xb_scr[...], ut_ref[...], preferred_element_type=F32)
    w = _gelu_exact(a).astype(BF16) * gate_scr[kstep]
    acc[...] += jnp.dot(w, v_ref[...], preferred_element_type=F32)

    @pl.when(kstep == pl.num_programs(1) - 1)
    def _():
        o_ref[...] = _layernorm(alpha * x_ref[...] + acc[...], ln_g_ref[...], ln_b_ref[...])


def _peer_experts(x1, gates, cidx, jidx, ut_bf16, v_bf16, ln_g, ln_b, *, alpha, n_keys, tm, ec):
    t, d_model = x1.shape
    n_exp = v_bf16.shape[0]
    hk = gates.shape[1]
    n_chunks = n_exp // ec
    return pl.pallas_call(
        functools.partial(_experts_kernel, alpha, n_keys),
        grid=(t // tm, n_chunks),
        in_specs=[
            pl.BlockSpec((tm, d_model), lambda i, k: (i, 0)),
            pl.BlockSpec((tm, hk), lambda i, k: (i, 0)),
            pl.BlockSpec((tm, hk), lambda i, k: (i, 0)),
            pl.BlockSpec((tm, hk), lambda i, k: (i, 0)),
            pl.BlockSpec((d_model, ec), lambda i, k: (0, k)),
            pl.BlockSpec((ec, d_model), lambda i, k: (k, 0)),
            pl.BlockSpec((1, d_model), lambda i, k: (0, 0)),
            pl.BlockSpec((1, d_model), lambda i, k: (0, 0)),
        ],
        out_specs=pl.BlockSpec((tm, d_model), lambda i, k: (i, 0)),
        out_shape=jax.ShapeDtypeStruct((t, d_model), F32),
        scratch_shapes=[
            pltpu.VMEM((n_chunks, tm, ec), BF16),
            pltpu.VMEM((tm, d_model), F32),
            pltpu.VMEM((tm, d_model), BF16),
            pltpu.VMEM((_GATE_GROUP * n_keys, n_keys), F32),
        ],
        compiler_params=pltpu.CompilerParams(
            dimension_semantics=("arbitrary", "arbitrary"),
            vmem_limit_bytes=V7X_VMEM_LIMIT_BYTES),
        name="peer_experts",
    )(x1, gates, cidx, jidx, ut_bf16, v_bf16, ln_g, ln_b)


def _largest_tile(n, cap, mult):
    best = None
    for c in range(mult, min(n, cap) + 1, mult):
        if n % c == 0:
            best = c
    assert best is not None, (n, cap, mult)
    return best


def _block_diag(w):
    h, a, b = w.shape
    return jnp.einsum("hij,hg->higj", w, jnp.eye(h, dtype=w.dtype)).reshape(h * a, h * b)


def kernel(x_prompt, x_sample, mem_prompt, cache_mem_k, cache_mem_v, state_lru_conv, state_lru_h, state_sc_conv, w_in, lru_conv_w, lru_conv_b, w_rg_a, b_rg_a, w_rg_x, b_rg_x, lru_lambda, sc_conv_w, w_mem_kv, g_lru, g_sc, g_mem, w_out, ln1_g, ln1_b, peer_wq, peer_keys, peer_u, peer_v, ln2_g, ln2_b):
    depth = w_in.shape[0]
    alpha = (2.0 * depth) ** 0.25
    bsz, seq, d_model = x_prompt.shape
    dbsz, dseq, _ = x_sample.shape
    n_mem, mem_heads, mem_hd = cache_mem_k.shape[2:]
    d_mem = mem_heads * mem_hd
    d_lru = state_lru_h.shape[-1]
    d_sc = state_sc_conv.shape[-1]
    n_heads, _, n_keys, dk2 = peer_keys.shape[1:]
    assert n_keys == V7X_LANES and dk2 == V7X_LANES
    t_p, t_s = bsz * seq, dbsz * dseq
    t_all = t_p + t_s

    xp, xs = x_prompt, x_sample
    outs = {k: [] for k in ("mk", "mv", "lc_p", "lh_p", "sc_p", "lc_s", "lh_s", "sc_s")}
    row = lambda v: v.reshape(1, -1)
    for l in range(depth):
        wts = dict(
            w_in=w_in[l].astype(BF16), lru_conv_w=lru_conv_w[l], lru_conv_b=row(lru_conv_b[l]),
            wa=_block_diag(w_rg_a[l]).astype(BF16), ba=row(b_rg_a[l]),
            wx=_block_diag(w_rg_x[l]).astype(BF16), bx=row(b_rg_x[l]), lam=row(lru_lambda[l]),
            sc_conv_w=sc_conv_w[l], g_lru=row(g_lru[l]), g_sc=row(g_sc[l]), g_mem=row(g_mem[l]),
            w_out=w_out[l].astype(BF16), ln_g=row(ln1_g[l]), ln_b=row(ln1_b[l]))
        mk, mv = _memkv(mem_prompt, w_mem_kv[l].astype(BF16))
        mix = functools.partial(_mixer, wts=wts, alpha=alpha, n_heads_mem=mem_heads)
        x1p, lc_p, lh_p, sc_p = mix(
            xp, mk, mv,
            jnp.zeros((bsz,) + state_lru_conv.shape[2:], F32), jnp.zeros((bsz, d_lru), F32),
            jnp.zeros((bsz,) + state_sc_conv.shape[2:], F32),
            nb=bsz, ts=_largest_tile(seq, 128, V7X_SUBLANES))
        x1s, lc_s, lh_s, sc_s = mix(
            xs, cache_mem_k[l].reshape(dbsz, n_mem, d_mem), cache_mem_v[l].reshape(dbsz, n_mem, d_mem),
            state_lru_conv[l], state_lru_h[l], state_sc_conv[l],
            nb=_largest_tile(dbsz, 16, V7X_SUBLANES), ts=dseq)

        x1 = jnp.concatenate([x1p.reshape(t_p, d_model), x1s.reshape(t_s, d_model)], axis=0)
        wq_heads = peer_wq[l].astype(BF16).reshape(d_model, n_heads, 2 * dk2).transpose(1, 0, 2)
        gates, cidx, jidx = _peer_topk(x1, wq_heads, peer_keys[l].astype(BF16),
                                       tm=_largest_tile(t_all, 256, V7X_LANES))
        x2 = _peer_experts(x1, gates, cidx, jidx, peer_u[l].T.astype(BF16), peer_v[l].astype(BF16),
                           row(ln2_g[l]), row(ln2_b[l]), alpha=alpha, n_keys=n_keys,
                           tm=_largest_tile(t_all, 512, _GATE_GROUP), ec=8 * n_keys)
        xp = x2[:t_p].reshape(bsz, seq, d_model)
        xs = x2[t_p:].reshape(dbsz, dseq, d_model)

        outs["mk"].append(mk.reshape(bsz, n_mem, mem_heads, mem_hd))
        outs["mv"].append(mv.reshape(bsz, n_mem, mem_heads, mem_hd))
        for name, val in (("lc_p", lc_p), ("lh_p", lh_p), ("sc_p", sc_p),
                          ("lc_s", lc_s), ("lh_s", lh_s), ("sc_s", sc_s)):
            outs[name].append(val)
    st = {k: jnp.stack(v) for k, v in outs.items()}
    return (xp, xs, st["mk"], st["mv"], st["lc_p"], st["lh_p"], st["sc_p"],
            st["lc_s"], st["lh_s"], st["sc_s"])
```

```python
import functools
import math

import jax
import jax.numpy as jnp
from jax import lax
from jax.experimental import pallas as pl
from jax.experimental.pallas import tpu as pltpu

F32 = jnp.float32
BF16 = jnp.bfloat16

LRU_C = 8.0
LN_EPS = 1e-5
RMS_EPS = 1e-6
PEER_TOPK = 16

V7X_LANES = 128
V7X_SUBLANES = 8
V7X_VMEM_LIMIT_BYTES = 56 * 1024 * 1024

_NT = (((1,), (1,)), ((), ()))


def _gelu_exact(x):
    return 0.5 * x * (1.0 + lax.erf(x * (1.0 / math.sqrt(2.0))))


def _layernorm(x, g, b):
    mu = jnp.mean(x, axis=-1, keepdims=True)
    xc = x - mu
    var = jnp.mean(xc * xc, axis=-1, keepdims=True)
    return xc * lax.rsqrt(var + LN_EPS) * g + b


def _rmsnorm(x, g):
    return x * lax.rsqrt(jnp.mean(x * x, axis=-1, keepdims=True) + RMS_EPS) * g


def _memkv_kernel(mem_ref, w_ref, k_ref, v_ref):
    d_mem = k_ref.shape[-1]
    kv = jnp.dot(mem_ref[0].astype(BF16), w_ref[...], preferred_element_type=F32)
    k_ref[0] = kv[:, :d_mem]
    v_ref[0] = kv[:, d_mem:]


def _memkv(mem, w_kv_bf16):
    bsz, n_mem, d_model = mem.shape
    d_mem = w_kv_bf16.shape[1] // 2
    out = jax.ShapeDtypeStruct((bsz, n_mem, d_mem), F32)
    return pl.pallas_call(
        _memkv_kernel,
        grid=(bsz,),
        in_specs=[
            pl.BlockSpec((1, n_mem, d_model), lambda b: (b, 0, 0)),
            pl.BlockSpec((d_model, 2 * d_mem), lambda b: (0, 0)),
        ],
        out_specs=[
            pl.BlockSpec((1, n_mem, d_mem), lambda b: (b, 0, 0)),
            pl.BlockSpec((1, n_mem, d_mem), lambda b: (b, 0, 0)),
        ],
        out_shape=[out, out],
        name="memkv",
    )(mem, w_kv_bf16)


_CONV_PAD = V7X_SUBLANES


def _causal_conv(buf_ref, x3, w_ref, width):
    nb, ts, c = x3.shape
    buf_ref[:, _CONV_PAD:_CONV_PAD + ts, :] = x3
    acc = x3 * w_ref[width - 1:width, :].reshape(1, 1, c)
    for k in range(width - 1):
        start = _CONV_PAD - (width - 1) + k
        acc = acc + buf_ref[:, start:start + ts, :] * w_ref[k:k + 1, :].reshape(1, 1, c)
    tail = buf_ref[:, _CONV_PAD + ts - (width - 1):_CONV_PAD + ts, :]
    buf_ref[:, _CONV_PAD - (width - 1):_CONV_PAD, :] = tail
    return acc, tail


def _mixer_kernel(alpha, n_heads_mem,
                  x_ref, mk_ref, mv_ref, lc0_ref, h0_ref, sc0_ref,
                  w_in_ref, cw_ref, cb_ref, wa_ref, ba_ref, wx_ref, bx_ref, lam_ref,
                  scw_ref, g_lru_ref, g_sc_ref, g_mem_ref, w_out_ref, ln_g_ref, ln_b_ref,
                  y_ref, lc_ref, h_ref, sc_ref,
                  cbuf_a, cbuf_b, a_scr, b_scr, hall, hcar, q_scr, ymem):
    nb, ts, d_model = x_ref.shape
    rows = nb * ts
    d_lru = h_ref.shape[-1]
    d_sc = sc_ref.shape[-1]
    d_mem = mk_ref.shape[-1]
    lru_w = cw_ref.shape[0]
    sc_w = scw_ref.shape[0]
    j = pl.program_id(1)

    @pl.when(j == 0)
    def _():
        cbuf_a[:, _CONV_PAD - (lru_w - 1):_CONV_PAD, :] = lc0_ref[...]
        cbuf_b[:, _CONV_PAD - (sc_w - 1):_CONV_PAD, :] = sc0_ref[...]
        hcar[...] = h0_ref[...]

    x2 = x_ref[...].reshape(rows, d_model)
    proj = jnp.dot(x2.astype(BF16), w_in_ref[...], preferred_element_type=F32)
    o = 0
    xl = proj[:, o:o + d_lru]; o += d_lru
    gate = proj[:, o:o + d_lru]; o += d_lru
    sc_b = proj[:, o:o + d_sc]; o += d_sc
    sc_c = proj[:, o:o + d_sc]; o += d_sc
    sc_x = proj[:, o:o + d_sc]; o += d_sc
    q_scr[...] = proj[:, o:o + d_mem]

    xc3, lc_tail = _causal_conv(cbuf_a, xl.reshape(nb, ts, d_lru), cw_ref, lru_w)
    lc_ref[...] = lc_tail
    xc = xc3.reshape(rows, d_lru) + cb_ref[...]
    xcb = xc.astype(BF16)
    r = jax.nn.sigmoid(jnp.dot(xcb, wa_ref[...], preferred_element_type=F32) + ba_ref[...])
    i = jax.nn.sigmoid(jnp.dot(xcb, wx_ref[...], preferred_element_type=F32) + bx_ref[...])
    lam = lam_ref[...]
    log_sig = -(jnp.maximum(-lam, 0.0) + jnp.log1p(jnp.exp(-jnp.abs(lam))))
    log_a = LRU_C * r * log_sig
    a = jnp.exp(log_a)
    bt = jnp.sqrt(-jnp.tanh(log_a) * (a * a + 1.0)) * (i * xc)
    n_lt = d_lru // V7X_LANES
    for c in range(n_lt):
        a_scr[c] = a[:, c * V7X_LANES:(c + 1) * V7X_LANES]
        b_scr[c] = bt[:, c * V7X_LANES:(c + 1) * V7X_LANES]

    def scan_step(s, hs):
        rows_s = pl.ds(s, nb, stride=ts)
        new = []
        for c in range(n_lt):
            h = a_scr[c, rows_s, :] * hs[c] + b_scr[c, rows_s, :]
            hall[c, rows_s, :] = h
            new.append(h)
        return tuple(new)

    h0 = hcar[...]
    hs = lax.fori_loop(0, ts, scan_step,
                       tuple(h0[:, c * V7X_LANES:(c + 1) * V7X_LANES] for c in range(n_lt)))
    h_last = jnp.concatenate(hs, axis=-1)
    hcar[...] = h_last
    h_ref[...] = h_last
    y_lru = jnp.concatenate([hall[c] for c in range(n_lt)], axis=-1) * _gelu_exact(gate)

    u3, sc_tail = _causal_conv(cbuf_b, (sc_c * sc_x).reshape(nb, ts, d_sc), scw_ref, sc_w)
    sc_ref[...] = sc_tail
    y_sc = sc_b * u3.reshape(rows, d_sc)

    hd = d_mem // n_heads_mem
    head_of_lane = lax.broadcasted_iota(jnp.int32, (ts, d_mem), 1) >> int(math.log2(hd))
    scale = hd ** -0.5

    def attn_step(b, carry):
        rws = pl.ds(pl.multiple_of(b * ts, ts), ts)
        qb = q_scr[rws, :]
        qm = jnp.concatenate(
            [jnp.where(head_of_lane == h, qb, 0.0) for h in range(n_heads_mem)], axis=0)
        kb = mk_ref[b].astype(BF16)
        vb = mv_ref[b].astype(BF16)
        s = lax.dot_general(qm.astype(BF16), kb, _NT, preferred_element_type=F32) * scale
        s = s - jnp.max(s, axis=-1, keepdims=True)
        p = jnp.exp(s)
        p = p / jnp.sum(p, axis=-1, keepdims=True)
        yh = jnp.dot(p.astype(BF16), vb, preferred_element_type=F32)
        yb = jnp.zeros((ts, d_mem), F32)
        for h in range(n_heads_mem):
            yb = yb + jnp.where(head_of_lane == h, yh[h * ts:(h + 1) * ts, :], 0.0)
        ymem[rws, :] = yb
        return carry

    lax.fori_loop(0, nb, attn_step, 0)

    y = jnp.concatenate(
        [_rmsnorm(y_lru, g_lru_ref[...]), _rmsnorm(y_sc, g_sc_ref[...]),
         _rmsnorm(ymem[...], g_mem_ref[...])], axis=-1)
    out = jnp.dot(y.astype(BF16), w_out_ref[...], preferred_element_type=F32)
    y_ref[...] = _layernorm(alpha * x2 + out, ln_g_ref[...], ln_b_ref[...]).reshape(nb, ts, d_model)


def _const_spec(shape):
    nd = len(shape)
    return pl.BlockSpec(shape, lambda i, j: (0,) * nd)


def _mixer(x, mem_k, mem_v, lc0, h0, sc0, wts, *, alpha, n_heads_mem, nb, ts):
    bsz, seq, d_model = x.shape
    n_mem, d_mem = mem_k.shape[1:]
    d_lru = h0.shape[-1]
    d_sc = sc0.shape[-1]
    lru_w = lc0.shape[1] + 1
    sc_w = sc0.shape[1] + 1
    rows = nb * ts
    grid = (bsz // nb, seq // ts)
    weights = [wts[k] for k in ("w_in", "lru_conv_w", "lru_conv_b", "wa", "ba", "wx", "bx", "lam",
                                "sc_conv_w", "g_lru", "g_sc", "g_mem", "w_out", "ln_g", "ln_b")]
    in_specs = [
        pl.BlockSpec((nb, ts, d_model), lambda i, j: (i, j, 0)),
        pl.BlockSpec((nb, n_mem, d_mem), lambda i, j: (i, 0, 0)),
        pl.BlockSpec((nb, n_mem, d_mem), lambda i, j: (i, 0, 0)),
        pl.BlockSpec((nb, lru_w - 1, d_lru), lambda i, j: (i, 0, 0)),
        pl.BlockSpec((nb, d_lru), lambda i, j: (i, 0)),
        pl.BlockSpec((nb, sc_w - 1, d_sc), lambda i, j: (i, 0, 0)),
    ] + [_const_spec(w.shape) for w in weights]
    out_specs = [
        pl.BlockSpec((nb, ts, d_model), lambda i, j: (i, j, 0)),
        pl.BlockSpec((nb, lru_w - 1, d_lru), lambda i, j: (i, 0, 0)),
        pl.BlockSpec((nb, d_lru), lambda i, j: (i, 0)),
        pl.BlockSpec((nb, sc_w - 1, d_sc), lambda i, j: (i, 0, 0)),
    ]
    out_shape = [
        jax.ShapeDtypeStruct((bsz, seq, d_model), F32),
        jax.ShapeDtypeStruct((bsz, lru_w - 1, d_lru), F32),
        jax.ShapeDtypeStruct((bsz, d_lru), F32),
        jax.ShapeDtypeStruct((bsz, sc_w - 1, d_sc), F32),
    ]
    scratch = [
        pltpu.VMEM((nb, ts + _CONV_PAD, d_lru), F32),
        pltpu.VMEM((nb, ts + _CONV_PAD, d_sc), F32),
        pltpu.VMEM((d_lru // V7X_LANES, rows, V7X_LANES), F32),
        pltpu.VMEM((d_lru // V7X_LANES, rows, V7X_LANES), F32),
        pltpu.VMEM((d_lru // V7X_LANES, rows, V7X_LANES), F32),
        pltpu.VMEM((nb, d_lru), F32),
        pltpu.VMEM((rows, d_mem), F32),
        pltpu.VMEM((rows, d_mem), F32),
    ]
    return pl.pallas_call(
        functools.partial(_mixer_kernel, alpha, n_heads_mem),
        grid=grid,
        in_specs=in_specs,
        out_specs=out_specs,
        out_shape=out_shape,
        scratch_shapes=scratch,
        compiler_params=pltpu.CompilerParams(
            dimension_semantics=("arbitrary", "arbitrary"),
            vmem_limit_bytes=V7X_VMEM_LIMIT_BYTES),
        name="mixer",
    )(x, mem_k, mem_v, lc0, h0, sc0, *weights)


def _top16_rows(vals, keys, big_key):
    out_v, out_k = [], []
    for _ in range(PEER_TOPK):
        m = jnp.max(vals, axis=0, keepdims=True)
        kk = jnp.min(jnp.where(vals == m, keys, big_key), axis=0, keepdims=True)
        vals = jnp.where(keys == kk, -jnp.inf, vals)
        out_v.append(m)
        out_k.append(kk)
    return out_v, out_k


def _candidate_blocks():
    k = PEER_TOPK
    blocks = []
    for a in range(V7X_SUBLANES):
        nvalid = k // (a + 1)
        for b0 in range(0, nvalid, V7X_SUBLANES):
            blocks.append(("row", a, b0, min(V7X_SUBLANES, nvalid - b0)))
    for a0 in range(V7X_SUBLANES, k, V7X_SUBLANES):
        blocks.append(("col", a0, 0, V7X_SUBLANES))
    return blocks


def _topk_kernel(n_keys, x_ref, wq_ref, keys_ref, g_ref, c_ref, j_ref,
                 v_scr, i_scr, s_scr, e_scr):
    tm = x_ref.shape[0]
    n_heads = wq_ref.shape[0]
    lanes = V7X_LANES
    k = PEER_TOPK
    xb = x_ref[...].astype(BF16)
    key_iota = lax.broadcasted_iota(jnp.int32, (n_keys, lanes), 0)
    sub_iota = lax.broadcasted_iota(jnp.int32, (V7X_SUBLANES, lanes), 0)
    n_experts_pad = n_keys * n_keys

    def head_body(h, carry):
        qh = jnp.dot(xb, wq_ref[h], preferred_element_type=F32)
        dk2 = qh.shape[1] // 2
        for p in range(2):
            qhp = qh[:, p * dk2:(p + 1) * dk2].astype(BF16)
            s_t = lax.dot_general(keys_ref[h, p], qhp, _NT, preferred_element_type=F32)
            for half in range(tm // lanes):
                vs, ks = _top16_rows(s_t[:, half * lanes:(half + 1) * lanes], key_iota, n_keys)
                for r in range(k):
                    v_scr[p, half, r:r + 1, :] = vs[r]
                    i_scr[p, half, r:r + 1, :] = ks[r]
        for half in range(tm // lanes):
            v1 = v_scr[0, half]
            v2 = v_scr[1, half]
            i1 = i_scr[0, half]
            i2 = i_scr[1, half]
            cand, ckey = [], []
            for kind, a0, b0, nvalid in _candidate_blocks():
                if kind == "row":
                    val = v1[a0:a0 + 1, :] + v2[b0:b0 + V7X_SUBLANES, :]
                    e = i1[a0:a0 + 1, :] * n_keys + i2[b0:b0 + V7X_SUBLANES, :]
                    ci = a0 * k + b0 + sub_iota
                    if nvalid < V7X_SUBLANES:
                        val = jnp.where(sub_iota < nvalid, val, -jnp.inf)
                else:
                    val = v1[a0:a0 + V7X_SUBLANES, :] + v2[0:1, :]
                    e = i1[a0:a0 + V7X_SUBLANES, :] * n_keys + i2[0:1, :]
                    ci = (a0 + sub_iota) * k
                cand.append(val)
                ckey.append(ci * n_experts_pad + e)
            ts_rows, tk_rows = _top16_rows(jnp.concatenate(cand, axis=0),
                                           jnp.concatenate(ckey, axis=0),
                                           k * k * n_experts_pad)
            for r in range(k):
                s_scr[h, r:r + 1, half * lanes:(half + 1) * lanes] = ts_rows[r]
                e_scr[h, r:r + 1, half * lanes:(half + 1) * lanes] = tk_rows[r] & (n_experts_pad - 1)
        return carry

    lax.fori_loop(0, n_heads, head_body, 0)

    s_all = s_scr[...]
    ex = jnp.exp(s_all - s_all[:, 0:1, :])
    gates = ex / jnp.sum(ex, axis=1, keepdims=True)
    e_all = e_scr[...]
    g_ref[...] = gates.reshape(n_heads * k, tm).T
    c_ref[...] = (e_all >> int(math.log2(n_keys))).astype(F32).reshape(n_heads * k, tm).T
    j_ref[...] = (e_all & (n_keys - 1)).astype(F32).reshape(n_heads * k, tm).T


def _peer_topk(x1, wq_heads, keys_bf16, *, tm):
    t, d_model = x1.shape
    n_heads, _, dk = wq_heads.shape
    n_keys = keys_bf16.shape[2]
    hk = n_heads * PEER_TOPK
    halves = tm // V7X_LANES
    out = jax.ShapeDtypeStruct((t, hk), F32)
    return pl.pallas_call(
        functools.partial(_topk_kernel, n_keys),
        grid=(t // tm,),
        in_specs=[
            pl.BlockSpec((tm, d_model), lambda i: (i, 0)),
            pl.BlockSpec(wq_heads.shape, lambda i: (0, 0, 0)),
            pl.BlockSpec(keys_bf16.shape, lambda i: (0, 0, 0, 0)),
        ],
        out_specs=[pl.BlockSpec((tm, hk), lambda i: (i, 0))] * 3,
        out_shape=[out, out, out],
        scratch_shapes=[
            pltpu.VMEM((2, halves, PEER_TOPK, V7X_LANES), F32),
            pltpu.VMEM((2, halves, PEER_TOPK, V7X_LANES), jnp.int32),
            pltpu.VMEM((n_heads, PEER_TOPK, tm), F32),
            pltpu.VMEM((n_heads, PEER_TOPK, tm), jnp.int32),
        ],
        compiler_params=pltpu.CompilerParams(
            dimension_semantics=("arbitrary",),
            vmem_limit_bytes=V7X_VMEM_LIMIT_BYTES),
        name="peer_topk",
    )(x1, wq_heads, keys_bf16)


_GATE_GROUP = 2 * V7X_SUBLANES


def _experts_kernel(alpha, n_keys, x_ref, g_ref, c_ref, j_ref, ut_ref, v_ref, ln_g_ref, ln_b_ref,
                    o_ref, gate_scr, acc, xb_scr):
    tm, d_model = x_ref.shape
    ec = ut_ref.shape[1]
    chunks_c = ec // n_keys
    kstep = pl.program_id(1)

    @pl.when(kstep == 0)
    def _():
        xb_scr[...] = x_ref[...].astype(BF16)
        acc[...] = jnp.zeros_like(acc)
        row_iota = lax.broadcasted_iota(jnp.int32, (n_keys, g_ref.shape[1]), 0).astype(F32)

        def group_body(tg, carry):
            base = pl.multiple_of(tg * _GATE_GROUP, _GATE_GROUP)
            per_token = []
            for tl in range(_GATE_GROUP):
                row = pl.ds(base + tl, 1)
                ct = jnp.where(c_ref[row, :] == row_iota, g_ref[row, :], 0.0).astype(BF16)
                jt = jnp.where(j_ref[row, :] == row_iota, 1.0, 0.0).astype(BF16)
                per_token.append(lax.dot_general(ct, jt, _NT, preferred_element_type=F32).astype(BF16))
            gate_scr[:, pl.ds(base, _GATE_GROUP), :] = jnp.transpose(jnp.stack(per_token), (1, 0, 2))
            return carry

        lax.fori_loop(0, tm // _GATE_GROUP, group_body, 0)

    a = jnp.dot(xb_scr[...], ut_ref[...], preferred_element_type=F32)
    gate = jnp.concatenate([gate_scr[kstep * chunks_c + i] for i in range(chunks_c)], axis=-1)
    w = _gelu_exact(a).astype(BF16) * gate
    acc[...] += jnp.dot(w, v_ref[...], preferred_element_type=F32)

    @pl.when(kstep == pl.num_programs(1) - 1)
    def _():
        o_ref[...] = _layernorm(alpha * x_ref[...] + acc[...], ln_g_ref[...], ln_b_ref[...])


def _peer_experts(x1, gates, cidx, jidx, ut_bf16, v_bf16, ln_g, ln_b, *, alpha, n_keys, tm, ec):
    t, d_model = x1.shape
    n_exp = v_bf16.shape[0]
    hk = gates.shape[1]
    n_chunks = n_exp // ec
    return pl.pallas_call(
        functools.partial(_experts_kernel, alpha, n_keys),
        grid=(t // tm, n_chunks),
        in_specs=[
            pl.BlockSpec((tm, d_model), lambda i, k: (i, 0)),
            pl.BlockSpec((tm, hk), lambda i, k: (i, 0)),
            pl.BlockSpec((tm, hk), lambda i, k: (i, 0)),
            pl.BlockSpec((tm, hk), lambda i, k: (i, 0)),
            pl.BlockSpec((d_model, ec), lambda i, k: (0, k)),
            pl.BlockSpec((ec, d_model), lambda i, k: (k, 0)),
            pl.BlockSpec((1, d_model), lambda i, k: (0, 0)),
            pl.BlockSpec((1, d_model), lambda i, k: (0, 0)),
        ],
        out_specs=pl.BlockSpec((tm, d_model), lambda i, k: (i, 0)),
        out_shape=jax.ShapeDtypeStruct((t, d_model), F32),
        scratch_shapes=[
            pltpu.VMEM((n_keys, tm, n_keys), BF16),
            pltpu.VMEM((tm, d_model), F32),
            pltpu.VMEM((tm, d_model), BF16),
        ],
        compiler_params=pltpu.CompilerParams(
            dimension_semantics=("arbitrary", "arbitrary"),
            vmem_limit_bytes=V7X_VMEM_LIMIT_BYTES),
        name="peer_experts",
    )(x1, gates, cidx, jidx, ut_bf16, v_bf16, ln_g, ln_b)


def _largest_tile(n, cap, mult):
    best = None
    for c in range(mult, min(n, cap) + 1, mult):
        if n % c == 0:
            best = c
    assert best is not None, (n, cap, mult)
    return best


def _block_diag(w):
    h, a, b = w.shape
    return jnp.einsum("hij,hg->higj", w, jnp.eye(h, dtype=w.dtype)).reshape(h * a, h * b)


def kernel(x_prompt, x_sample, mem_prompt, cache_mem_k, cache_mem_v, state_lru_conv, state_lru_h, state_sc_conv, w_in, lru_conv_w, lru_conv_b, w_rg_a, b_rg_a, w_rg_x, b_rg_x, lru_lambda, sc_conv_w, w_mem_kv, g_lru, g_sc, g_mem, w_out, ln1_g, ln1_b, peer_wq, peer_keys, peer_u, peer_v, ln2_g, ln2_b):
    depth = w_in.shape[0]
    alpha = (2.0 * depth) ** 0.25
    bsz, seq, d_model = x_prompt.shape
    dbsz, dseq, _ = x_sample.shape
    n_mem, mem_heads, mem_hd = cache_mem_k.shape[2:]
    d_mem = mem_heads * mem_hd
    d_lru = state_lru_h.shape[-1]
    d_sc = state_sc_conv.shape[-1]
    n_heads, _, n_keys, dk2 = peer_keys.shape[1:]
    assert n_keys == V7X_LANES and dk2 == V7X_LANES
    t_p, t_s = bsz * seq, dbsz * dseq
    t_all = t_p + t_s

    xp, xs = x_prompt, x_sample
    outs = {k: [] for k in ("mk", "mv", "lc_p", "lh_p", "sc_p", "lc_s", "lh_s", "sc_s")}
    row = lambda v: v.reshape(1, -1)
    for l in range(depth):
        wts = dict(
            w_in=w_in[l].astype(BF16), lru_conv_w=lru_conv_w[l], lru_conv_b=row(lru_conv_b[l]),
            wa=_block_diag(w_rg_a[l]).astype(BF16), ba=row(b_rg_a[l]),
            wx=_block_diag(w_rg_x[l]).astype(BF16), bx=row(b_rg_x[l]), lam=row(lru_lambda[l]),
            sc_conv_w=sc_conv_w[l], g_lru=row(g_lru[l]), g_sc=row(g_sc[l]), g_mem=row(g_mem[l]),
            w_out=w_out[l].astype(BF16), ln_g=row(ln1_g[l]), ln_b=row(ln1_b[l]))
        mk, mv = _memkv(mem_prompt, w_mem_kv[l].astype(BF16))
        mix = functools.partial(_mixer, wts=wts, alpha=alpha, n_heads_mem=mem_heads)
        x1p, lc_p, lh_p, sc_p = mix(
            xp, mk, mv,
            jnp.zeros((bsz,) + state_lru_conv.shape[2:], F32), jnp.zeros((bsz, d_lru), F32),
            jnp.zeros((bsz,) + state_sc_conv.shape[2:], F32),
            nb=bsz, ts=_largest_tile(seq, 128, V7X_SUBLANES))
        x1s, lc_s, lh_s, sc_s = mix(
            xs, cache_mem_k[l].reshape(dbsz, n_mem, d_mem), cache_mem_v[l].reshape(dbsz, n_mem, d_mem),
            state_lru_conv[l], state_lru_h[l], state_sc_conv[l],
            nb=_largest_tile(dbsz, 16, V7X_SUBLANES), ts=dseq)

        x1 = jnp.concatenate([x1p.reshape(t_p, d_model), x1s.reshape(t_s, d_model)], axis=0)
        wq_heads = peer_wq[l].astype(BF16).reshape(d_model, n_heads, 2 * dk2).transpose(1, 0, 2)
        gates, cidx, jidx = _peer_topk(x1, wq_heads, peer_keys[l].astype(BF16),
                                       tm=_largest_tile(t_all, 256, V7X_LANES))
        x2 = _peer_experts(x1, gates, cidx, jidx, peer_u[l].T.astype(BF16), peer_v[l].astype(BF16),
                           row(ln2_g[l]), row(ln2_b[l]), alpha=alpha, n_keys=n_keys,
                           tm=_largest_tile(t_all, 512, _GATE_GROUP), ec=8 * n_keys)
        xp = x2[:t_p].reshape(bsz, seq, d_model)
        xs = x2[t_p:].reshape(dbsz, dseq, d_model)

        outs["mk"].append(mk.reshape(bsz, n_mem, mem_heads, mem_hd))
        outs["mv"].append(mv.reshape(bsz, n_mem, mem_heads, mem_hd))
        for name, val in (("lc_p", lc_p), ("lh_p", lh_p), ("sc_p", sc_p),
                          ("lc_s", lc_s), ("lh_s", lh_s), ("sc_s", sc_s)):
            outs[name].append(val)
    st = {k: jnp.stack(v) for k, v in outs.items()}
    return (xp, xs, st["mk"], st["mv"], st["lc_p"], st["lh_p"], st["sc_p"],
            st["lc_s"], st["lh_s"], st["sc_s"])
```

```python
import functools
import math

import jax
import jax.numpy as jnp
from jax import lax
from jax.experimental import pallas as pl
from jax.experimental.pallas import tpu as pltpu

F32 = jnp.float32
BF16 = jnp.bfloat16

LRU_C = 8.0
LN_EPS = 1e-5
RMS_EPS = 1e-6
PEER_TOPK = 16

V7X_LANES = 128
V7X_SUBLANES = 8
V7X_VMEM_LIMIT_BYTES = 56 * 1024 * 1024

_NT = (((1,), (1,)), ((), ()))


def _gelu_exact(x):
    return 0.5 * x * (1.0 + lax.erf(x * (1.0 / math.sqrt(2.0))))


def _layernorm(x, g, b):
    mu = jnp.mean(x, axis=-1, keepdims=True)
    xc = x - mu
    var = jnp.mean(xc * xc, axis=-1, keepdims=True)
    return xc * lax.rsqrt(var + LN_EPS) * g + b


def _rmsnorm(x, g):
    return x * lax.rsqrt(jnp.mean(x * x, axis=-1, keepdims=True) + RMS_EPS) * g


def _memkv_kernel(mem_ref, w_ref, k_ref, v_ref):
    d_mem = k_ref.shape[-1]
    kv = jnp.dot(mem_ref[0].astype(BF16), w_ref[...], preferred_element_type=F32)
    k_ref[0] = kv[:, :d_mem]
    v_ref[0] = kv[:, d_mem:]


def _memkv(mem, w_kv_bf16):
    bsz, n_mem, d_model = mem.shape
    d_mem = w_kv_bf16.shape[1] // 2
    out = jax.ShapeDtypeStruct((bsz, n_mem, d_mem), F32)
    return pl.pallas_call(
        _memkv_kernel,
        grid=(bsz,),
        in_specs=[
            pl.BlockSpec((1, n_mem, d_model), lambda b: (b, 0, 0)),
            pl.BlockSpec((d_model, 2 * d_mem), lambda b: (0, 0)),
        ],
        out_specs=[
            pl.BlockSpec((1, n_mem, d_mem), lambda b: (b, 0, 0)),
            pl.BlockSpec((1, n_mem, d_mem), lambda b: (b, 0, 0)),
        ],
        out_shape=[out, out],
        name="memkv",
    )(mem, w_kv_bf16)


_CONV_PAD = V7X_SUBLANES


def _causal_conv(buf_ref, x3, w_ref, width):
    nb, ts, c = x3.shape
    buf_ref[:, _CONV_PAD:_CONV_PAD + ts, :] = x3
    acc = x3 * w_ref[width - 1:width, :].reshape(1, 1, c)
    for k in range(width - 1):
        start = _CONV_PAD - (width - 1) + k
        acc = acc + buf_ref[:, start:start + ts, :] * w_ref[k:k + 1, :].reshape(1, 1, c)
    tail = buf_ref[:, _CONV_PAD + ts - (width - 1):_CONV_PAD + ts, :]
    buf_ref[:, _CONV_PAD - (width - 1):_CONV_PAD, :] = tail
    return acc, tail


def _mixer_kernel(alpha, n_heads_mem,
                  x_ref, mk_ref, mv_ref, lc0_ref, h0_ref, sc0_ref,
                  w_in_ref, cw_ref, cb_ref, wa_ref, ba_ref, wx_ref, bx_ref, lam_ref,
                  scw_ref, g_lru_ref, g_sc_ref, g_mem_ref, w_out_ref, ln_g_ref, ln_b_ref,
                  y_ref, lc_ref, h_ref, sc_ref,
                  cbuf_a, cbuf_b, a_scr, b_scr, hall, hcar, q_scr, ymem):
    nb, ts, d_model = x_ref.shape
    rows = nb * ts
    d_lru = h_ref.shape[-1]
    d_sc = sc_ref.shape[-1]
    d_mem = mk_ref.shape[-1]
    lru_w = cw_ref.shape[0]
    sc_w = scw_ref.shape[0]
    j = pl.program_id(1)

    @pl.when(j == 0)
    def _():
        cbuf_a[:, _CONV_PAD - (lru_w - 1):_CONV_PAD, :] = lc0_ref[...]
        cbuf_b[:, _CONV_PAD - (sc_w - 1):_CONV_PAD, :] = sc0_ref[...]
        hcar[...] = h0_ref[...]

    x2 = x_ref[...].reshape(rows, d_model)
    proj = jnp.dot(x2.astype(BF16), w_in_ref[...], preferred_element_type=F32)
    o = 0
    xl = proj[:, o:o + d_lru]; o += d_lru
    gate = proj[:, o:o + d_lru]; o += d_lru
    sc_b = proj[:, o:o + d_sc]; o += d_sc
    sc_c = proj[:, o:o + d_sc]; o += d_sc
    sc_x = proj[:, o:o + d_sc]; o += d_sc
    q_scr[...] = proj[:, o:o + d_mem]

    xc3, lc_tail = _causal_conv(cbuf_a, xl.reshape(nb, ts, d_lru), cw_ref, lru_w)
    lc_ref[...] = lc_tail
    xc = xc3.reshape(rows, d_lru) + cb_ref[...]
    xcb = xc.astype(BF16)
    r = jax.nn.sigmoid(jnp.dot(xcb, wa_ref[...], preferred_element_type=F32) + ba_ref[...])
    i = jax.nn.sigmoid(jnp.dot(xcb, wx_ref[...], preferred_element_type=F32) + bx_ref[...])
    lam = lam_ref[...]
    log_sig = -(jnp.maximum(-lam, 0.0) + jnp.log1p(jnp.exp(-jnp.abs(lam))))
    log_a = LRU_C * r * log_sig
    a = jnp.exp(log_a)
    bt = jnp.sqrt(-jnp.tanh(log_a) * (a * a + 1.0)) * (i * xc)
    n_lt = d_lru // V7X_LANES
    for c in range(n_lt):
        a_scr[c] = a[:, c * V7X_LANES:(c + 1) * V7X_LANES]
        b_scr[c] = bt[:, c * V7X_LANES:(c + 1) * V7X_LANES]

    def scan_step(s, hs):
        rows_s = pl.ds(s, nb, stride=ts)
        new = []
        for c in range(n_lt):
            h = a_scr[c, rows_s, :] * hs[c] + b_scr[c, rows_s, :]
            hall[c, rows_s, :] = h
            new.append(h)
        return tuple(new)

    h0 = hcar[...]
    hs = lax.fori_loop(0, ts, scan_step,
                       tuple(h0[:, c * V7X_LANES:(c + 1) * V7X_LANES] for c in range(n_lt)))
    h_last = jnp.concatenate(hs, axis=-1)
    hcar[...] = h_last
    h_ref[...] = h_last
    y_lru = jnp.concatenate([hall[c] for c in range(n_lt)], axis=-1) * _gelu_exact(gate)

    u3, sc_tail = _causal_conv(cbuf_b, (sc_c * sc_x).reshape(nb, ts, d_sc), scw_ref, sc_w)
    sc_ref[...] = sc_tail
    y_sc = sc_b * u3.reshape(rows, d_sc)

    hd = d_mem // n_heads_mem
    head_of_lane = lax.broadcasted_iota(jnp.int32, (ts, d_mem), 1) >> int(math.log2(hd))
    scale = hd ** -0.5

    def attn_step(b, carry):
        rws = pl.ds(pl.multiple_of(b * ts, ts), ts)
        qb = q_scr[rws, :]
        qm = jnp.concatenate(
            [jnp.where(head_of_lane == h, qb, 0.0) for h in range(n_heads_mem)], axis=0)
        kb = mk_ref[b].astype(BF16)
        vb = mv_ref[b].astype(BF16)
        s = lax.dot_general(qm.astype(BF16), kb, _NT, preferred_element_type=F32) * scale
        s = s - jnp.max(s, axis=-1, keepdims=True)
        p = jnp.exp(s)
        p = p / jnp.sum(p, axis=-1, keepdims=True)
        yh = jnp.dot(p.astype(BF16), vb, preferred_element_type=F32)
        yb = jnp.zeros((ts, d_mem), F32)
        for h in range(n_heads_mem):
            yb = yb + jnp.where(head_of_lane == h, yh[h * ts:(h + 1) * ts, :], 0.0)
        ymem[rws, :] = yb
        return carry

    lax.fori_loop(0, nb, attn_step, 0)

    y = jnp.concatenate(
        [_rmsnorm(y_lru, g_lru_ref[...]), _rmsnorm(y_sc, g_sc_ref[...]),
         _rmsnorm(ymem[...], g_mem_ref[...])], axis=-1)
    out = jnp.dot(y.astype(BF16), w_out_ref[...], preferred_element_type=F32)
    y_ref[...] = _layernorm(alpha * x2 + out, ln_g_ref[...], ln_b_ref[...]).reshape(nb, ts, d_model)


def _const_spec(shape):
    nd = len(shape)
    return pl.BlockSpec(shape, lambda i, j: (0,) * nd)


def _mixer(x, mem_k, mem_v, lc0, h0, sc0, wts, *, alpha, n_heads_mem, nb, ts):
    bsz, seq, d_model = x.shape
    n_mem, d_mem = mem_k.shape[1:]
    d_lru = h0.shape[-1]
    d_sc = sc0.shape[-1]
    lru_w = lc0.shape[1] + 1
    sc_w = sc0.shape[1] + 1
    rows = nb * ts
    grid = (bsz // nb, seq // ts)
    weights = [wts[k] for k in ("w_in", "lru_conv_w", "lru_conv_b", "wa", "ba", "wx", "bx", "lam",
                                "sc_conv_w", "g_lru", "g_sc", "g_mem", "w_out", "ln_g", "ln_b")]
    in_specs = [
        pl.BlockSpec((nb, ts, d_model), lambda i, j: (i, j, 0)),
        pl.BlockSpec((nb, n_mem, d_mem), lambda i, j: (i, 0, 0)),
        pl.BlockSpec((nb, n_mem, d_mem), lambda i, j: (i, 0, 0)),
        pl.BlockSpec((nb, lru_w - 1, d_lru), lambda i, j: (i, 0, 0)),
        pl.BlockSpec((nb, d_lru), lambda i, j: (i, 0)),
        pl.BlockSpec((nb, sc_w - 1, d_sc), lambda i, j: (i, 0, 0)),
    ] + [_const_spec(w.shape) for w in weights]
    out_specs = [
        pl.BlockSpec((nb, ts, d_model), lambda i, j: (i, j, 0)),
        pl.BlockSpec((nb, lru_w - 1, d_lru), lambda i, j: (i, 0, 0)),
        pl.BlockSpec((nb, d_lru), lambda i, j: (i, 0)),
        pl.BlockSpec((nb, sc_w - 1, d_sc), lambda i, j: (i, 0, 0)),
    ]
    out_shape = [
        jax.ShapeDtypeStruct((bsz, seq, d_model), F32),
        jax.ShapeDtypeStruct((bsz, lru_w - 1, d_lru), F32),
        jax.ShapeDtypeStruct((bsz, d_lru), F32),
        jax.ShapeDtypeStruct((bsz, sc_w - 1, d_sc), F32),
    ]
    scratch = [
        pltpu.VMEM((nb, ts + _CONV_PAD, d_lru), F32),
        pltpu.VMEM((nb, ts + _CONV_PAD, d_sc), F32),
        pltpu.VMEM((d_lru // V7X_LANES, rows, V7X_LANES), F32),
        pltpu.VMEM((d_lru // V7X_LANES, rows, V7X_LANES), F32),
        pltpu.VMEM((d_lru // V7X_LANES, rows, V7X_LANES), F32),
        pltpu.VMEM((nb, d_lru), F32),
        pltpu.VMEM((rows, d_mem), F32),
        pltpu.VMEM((rows, d_mem), F32),
    ]
    return pl.pallas_call(
        functools.partial(_mixer_kernel, alpha, n_heads_mem),
        grid=grid,
        in_specs=in_specs,
        out_specs=out_specs,
        out_shape=out_shape,
        scratch_shapes=scratch,
        compiler_params=pltpu.CompilerParams(
            dimension_semantics=("arbitrary", "arbitrary"),
            vmem_limit_bytes=V7X_VMEM_LIMIT_BYTES),
        name="mixer",
    )(x, mem_k, mem_v, lc0, h0, sc0, *weights)


def _argmax_tree(nodes):
    nodes = list(nodes)
    while len(nodes) > 1:
        nxt = []
        for i in range(0, len(nodes) - 1, 2):
            (av, ak), (bv, bk) = nodes[i], nodes[i + 1]
            nxt.append((jnp.maximum(av, bv), jnp.where(av >= bv, ak, bk)))
        if len(nodes) % 2:
            nxt.append(nodes[-1])
        nodes = nxt
    return nodes[0]


def _pair_list():
    k = PEER_TOPK
    return [(a, b) for a in range(k) for b in range(k // (a + 1))]


def _topk_kernel(n_keys, n_heads, x_ref, wq_ref, kb_ref, g_ref, c_ref, j_ref,
                 q_scr, s_scr, v_scr, i_scr, cv_scr, ck_scr, ts_scr, tk_scr):
    tm = x_ref.shape[0]
    lanes = V7X_LANES
    k = PEER_TOPK
    n_experts = n_keys * n_keys
    dk2 = kb_ref.shape[3]
    pairs = _pair_list()
    q_scr[...] = jnp.dot(x_ref[...].astype(BF16), wq_ref[...],
                         preferred_element_type=F32).astype(BF16)
    none = jnp.full((n_heads, lanes), -1.0, F32)

    def block_body(blk, carry):
        rows = pl.ds(pl.multiple_of(blk * lanes, lanes), lanes)
        q = q_scr[rows, :]
        for p in range(2):
            per_head = []
            for h in range(n_heads):
                col = (p * n_heads + h) * dk2
                per_head.append(lax.dot_general(kb_ref[p, h], q[:, col:col + dk2], _NT,
                                                preferred_element_type=F32))
            s_scr[p] = jnp.transpose(jnp.stack(per_head), (1, 0, 2))

        def rank_body(it, prev):
            new = []
            for p in range(2):
                leaves = []
                for key in range(n_keys):
                    v = jnp.where(prev[p] == float(key), -jnp.inf, s_scr[p, key])
                    s_scr[p, key] = v
                    leaves.append((v, float(key)))
                m, idx = _argmax_tree(leaves)
                v_scr[p, it] = m
                i_scr[p, it] = idx
                new.append(idx)
            return tuple(new)

        lax.fori_loop(0, k, rank_body, (none, none))

        for n, (a, b) in enumerate(pairs):
            cv_scr[n] = v_scr[0, a] + v_scr[1, b]
            ck_scr[n] = float((a * k + b) * n_experts) + (i_scr[0, a] * float(n_keys) + i_scr[1, b])

        def pair_body(it, prev):
            leaves = []
            for n in range(len(pairs)):
                kc = ck_scr[n]
                v = jnp.where(kc == prev, -jnp.inf, cv_scr[n])
                cv_scr[n] = v
                leaves.append((v, kc))
            m, kk = _argmax_tree(leaves)
            ts_scr[it] = m
            tk_scr[it] = kk
            return kk

        lax.fori_loop(0, k, pair_body, none)

        top = ts_scr[...]
        ex = jnp.exp(top - top[0:1])
        gates = ex / jnp.sum(ex, axis=0, keepdims=True)
        e = tk_scr[...].astype(jnp.int32) & (n_experts - 1)
        g_ref[rows, :] = gates.reshape(k * n_heads, lanes).T
        c_ref[rows, :] = (e >> int(math.log2(n_keys))).astype(F32).reshape(k * n_heads, lanes).T
        j_ref[rows, :] = (e & (n_keys - 1)).astype(F32).reshape(k * n_heads, lanes).T
        return carry

    lax.fori_loop(0, tm // lanes, block_body, 0)


def _peer_topk(x1, wq_perm, keys_t, *, n_heads, tm):
    t, d_model = x1.shape
    n_keys = keys_t.shape[2]
    hk = n_heads * PEER_TOPK
    n_pairs = len(_pair_list())
    out = jax.ShapeDtypeStruct((t, hk), F32)
    vreg = (n_heads, V7X_LANES)
    return pl.pallas_call(
        functools.partial(_topk_kernel, n_keys, n_heads),
        grid=(t // tm,),
        in_specs=[
            pl.BlockSpec((tm, d_model), lambda i: (i, 0)),
            pl.BlockSpec(wq_perm.shape, lambda i: (0, 0)),
            pl.BlockSpec(keys_t.shape, lambda i: (0, 0, 0, 0)),
        ],
        out_specs=[pl.BlockSpec((tm, hk), lambda i: (i, 0))] * 3,
        out_shape=[out, out, out],
        scratch_shapes=[
            pltpu.VMEM((tm, wq_perm.shape[1]), BF16),
            pltpu.VMEM((2, n_keys) + vreg, F32),
            pltpu.VMEM((2, PEER_TOPK) + vreg, F32),
            pltpu.VMEM((2, PEER_TOPK) + vreg, F32),
            pltpu.VMEM((n_pairs,) + vreg, F32),
            pltpu.VMEM((n_pairs,) + vreg, F32),
            pltpu.VMEM((PEER_TOPK,) + vreg, F32),
            pltpu.VMEM((PEER_TOPK,) + vreg, F32),
        ],
        compiler_params=pltpu.CompilerParams(
            dimension_semantics=("arbitrary",),
            vmem_limit_bytes=V7X_VMEM_LIMIT_BYTES),
        name="peer_topk",
    )(x1, wq_perm, keys_t)


_GATE_GROUP = 2 * V7X_SUBLANES


def _experts_kernel(alpha, n_keys, x_ref, g_ref, c_ref, j_ref, ut_ref, v_ref, ln_g_ref, ln_b_ref,
                    o_ref, gate_scr, acc, xb_scr):
    tm, d_model = x_ref.shape
    ec = ut_ref.shape[1]
    chunks_c = ec // n_keys
    kstep = pl.program_id(1)

    @pl.when(kstep == 0)
    def _():
        xb_scr[...] = x_ref[...].astype(BF16)
        acc[...] = jnp.zeros_like(acc)
        row_iota = lax.broadcasted_iota(jnp.int32, (n_keys, g_ref.shape[1]), 0).astype(F32)

        def group_body(tg, carry):
            base = pl.multiple_of(tg * _GATE_GROUP, _GATE_GROUP)
            per_token = []
            for tl in range(_GATE_GROUP):
                row = pl.ds(base + tl, 1)
                ct = jnp.where(c_ref[row, :] == row_iota, g_ref[row, :], 0.0).astype(BF16)
                jt = jnp.where(j_ref[row, :] == row_iota, 1.0, 0.0).astype(BF16)
                per_token.append(lax.dot_general(ct, jt, _NT, preferred_element_type=F32).astype(BF16))
            gate_scr[:, pl.ds(base, _GATE_GROUP), :] = jnp.transpose(jnp.stack(per_token), (1, 0, 2))
            return carry

        lax.fori_loop(0, tm // _GATE_GROUP, group_body, 0)

    a = jnp.dot(xb_scr[...], ut_ref[...], preferred_element_type=F32)
    gate = jnp.concatenate([gate_scr[kstep * chunks_c + i] for i in range(chunks_c)], axis=-1)
    w = _gelu_exact(a).astype(BF16) * gate
    acc[...] += jnp.dot(w, v_ref[...], preferred_element_type=F32)

    @pl.when(kstep == pl.num_programs(1) - 1)
    def _():
        o_ref[...] = _layernorm(alpha * x_ref[...] + acc[...], ln_g_ref[...], ln_b_ref[...])


def _peer_experts(x1, gates, cidx, jidx, ut_bf16, v_bf16, ln_g, ln_b, *, alpha, n_keys, tm, ec):
    t, d_model = x1.shape
    n_exp = v_bf16.shape[0]
    hk = gates.shape[1]
    n_chunks = n_exp // ec
    return pl.pallas_call(
        functools.partial(_experts_kernel, alpha, n_keys),
        grid=(t // tm, n_chunks),
        in_specs=[
            pl.BlockSpec((tm, d_model), lambda i, k: (i, 0)),
            pl.BlockSpec((tm, hk), lambda i, k: (i, 0)),
            pl.BlockSpec((tm, hk), lambda i, k: (i, 0)),
            pl.BlockSpec((tm, hk), lambda i, k: (i, 0)),
            pl.BlockSpec((d_model, ec), lambda i, k: (0, k)),
            pl.BlockSpec((ec, d_model), lambda i, k: (k, 0)),
            pl.BlockSpec((1, d_model), lambda i, k: (0, 0)),
            pl.BlockSpec((1, d_model), lambda i, k: (0, 0)),
        ],
        out_specs=pl.BlockSpec((tm, d_model), lambda i, k: (i, 0)),
        out_shape=jax.ShapeDtypeStruct((t, d_model), F32),
        scratch_shapes=[
            pltpu.VMEM((n_keys, tm, n_keys), BF16),
            pltpu.VMEM((tm, d_model), F32),
            pltpu.VMEM((tm, d_model), BF16),
        ],
        compiler_params=pltpu.CompilerParams(
            dimension_semantics=("arbitrary", "arbitrary"),
            vmem_limit_bytes=V7X_VMEM_LIMIT_BYTES),
        name="peer_experts",
    )(x1, gates, cidx, jidx, ut_bf16, v_bf16, ln_g, ln_b)


def _largest_tile(n, cap, mult):
    best = None
    for c in range(mult, min(n, cap) + 1, mult):
        if n % c == 0:
            best = c
    assert best is not None, (n, cap, mult)
    return best


def _block_diag(w):
    h, a, b = w.shape
    return jnp.einsum("hij,hg->higj", w, jnp.eye(h, dtype=w.dtype)).reshape(h * a, h * b)


def kernel(x_prompt, x_sample, mem_prompt, cache_mem_k, cache_mem_v, state_lru_conv, state_lru_h, state_sc_conv, w_in, lru_conv_w, lru_conv_b, w_rg_a, b_rg_a, w_rg_x, b_rg_x, lru_lambda, sc_conv_w, w_mem_kv, g_lru, g_sc, g_mem, w_out, ln1_g, ln1_b, peer_wq, peer_keys, peer_u, peer_v, ln2_g, ln2_b):
    depth = w_in.shape[0]
    alpha = (2.0 * depth) ** 0.25
    bsz, seq, d_model = x_prompt.shape
    dbsz, dseq, _ = x_sample.shape
    n_mem, mem_heads, mem_hd = cache_mem_k.shape[2:]
    d_mem = mem_heads * mem_hd
    d_lru = state_lru_h.shape[-1]
    d_sc = state_sc_conv.shape[-1]
    n_heads, _, n_keys, dk2 = peer_keys.shape[1:]
    assert n_keys == V7X_LANES and dk2 == V7X_LANES

    xp, xs = x_prompt, x_sample
    outs = {k: [] for k in ("mk", "mv", "lc_p", "lh_p", "sc_p", "lc_s", "lh_s", "sc_s")}
    row = lambda v: v.reshape(1, -1)
    for l in range(depth):
        wts = dict(
            w_in=w_in[l].astype(BF16), lru_conv_w=lru_conv_w[l], lru_conv_b=row(lru_conv_b[l]),
            wa=_block_diag(w_rg_a[l]).astype(BF16), ba=row(b_rg_a[l]),
            wx=_block_diag(w_rg_x[l]).astype(BF16), bx=row(b_rg_x[l]), lam=row(lru_lambda[l]),
            sc_conv_w=sc_conv_w[l], g_lru=row(g_lru[l]), g_sc=row(g_sc[l]), g_mem=row(g_mem[l]),
            w_out=w_out[l].astype(BF16), ln_g=row(ln1_g[l]), ln_b=row(ln1_b[l]))
        mk, mv = _memkv(mem_prompt, w_mem_kv[l].astype(BF16))
        mix = functools.partial(_mixer, wts=wts, alpha=alpha, n_heads_mem=mem_heads)
        x1p, lc_p, lh_p, sc_p = mix(
            xp, mk, mv,
            jnp.zeros((bsz,) + state_lru_conv.shape[2:], F32), jnp.zeros((bsz, d_lru), F32),
            jnp.zeros((bsz,) + state_sc_conv.shape[2:], F32),
            nb=bsz, ts=_largest_tile(seq, 128, V7X_SUBLANES))
        x1s, lc_s, lh_s, sc_s = mix(
            xs, cache_mem_k[l].reshape(dbsz, n_mem, d_mem), cache_mem_v[l].reshape(dbsz, n_mem, d_mem),
            state_lru_conv[l], state_lru_h[l], state_sc_conv[l],
            nb=_largest_tile(dbsz, 16, V7X_SUBLANES), ts=dseq)

        wq_perm = peer_wq[l].astype(BF16).reshape(d_model, n_heads, 2, dk2).transpose(0, 2, 1, 3)
        wq_perm = wq_perm.reshape(d_model, 2 * n_heads * dk2)
        keys_t = peer_keys[l].astype(BF16).transpose(1, 0, 2, 3)
        ut_bf16 = peer_u[l].T.astype(BF16)
        v_bf16 = peer_v[l].astype(BF16)

        def peer(x1_3d):
            x1 = x1_3d.reshape(-1, d_model)
            t = x1.shape[0]
            gates, cidx, jidx = _peer_topk(x1, wq_perm, keys_t, n_heads=n_heads,
                                           tm=_largest_tile(t, 512, V7X_LANES))
            x2 = _peer_experts(x1, gates, cidx, jidx, ut_bf16, v_bf16, row(ln2_g[l]), row(ln2_b[l]),
                               alpha=alpha, n_keys=n_keys, tm=_largest_tile(t, 512, _GATE_GROUP),
                               ec=8 * n_keys)
            return x2.reshape(x1_3d.shape)

        xp = peer(x1p)
        xs = peer(x1s)

        outs["mk"].append(mk.reshape(bsz, n_mem, mem_heads, mem_hd))
        outs["mv"].append(mv.reshape(bsz, n_mem, mem_heads, mem_hd))
        for name, val in (("lc_p", lc_p), ("lh_p", lh_p), ("sc_p", sc_p),
                          ("lc_s", lc_s), ("lh_s", lh_s), ("sc_s", sc_s)):
            outs[name].append(val)
    st = {k: jnp.stack(v) for k, v in outs.items()}
    return (xp, xs, st["mk"], st["mv"], st["lc_p"], st["lh_p"], st["sc_p"],
            st["lc_s"], st["lh_s"], st["sc_s"])
```

```python
import functools
import math

import jax
import jax.numpy as jnp
from jax import lax
from jax.experimental import pallas as pl
from jax.experimental.pallas import tpu as pltpu

F32 = jnp.float32
BF16 = jnp.bfloat16

LRU_C = 8.0
LN_EPS = 1e-5
RMS_EPS = 1e-6
PEER_TOPK = 16

V7X_LANES = 128
V7X_SUBLANES = 8
V7X_VMEM_LIMIT_BYTES = 56 * 1024 * 1024

_NT = (((1,), (1,)), ((), ()))


def _gelu_exact(x):
    return 0.5 * x * (1.0 + lax.erf(x * (1.0 / math.sqrt(2.0))))


def _layernorm(x, g, b):
    mu = jnp.mean(x, axis=-1, keepdims=True)
    xc = x - mu
    var = jnp.mean(xc * xc, axis=-1, keepdims=True)
    return xc * lax.rsqrt(var + LN_EPS) * g + b


def _rmsnorm(x, g):
    return x * lax.rsqrt(jnp.mean(x * x, axis=-1, keepdims=True) + RMS_EPS) * g


def _memkv_kernel(mem_ref, w_ref, k_ref, v_ref):
    d_mem = k_ref.shape[-1]
    kv = jnp.dot(mem_ref[0].astype(BF16), w_ref[...], preferred_element_type=F32)
    k_ref[0] = kv[:, :d_mem]
    v_ref[0] = kv[:, d_mem:]


def _memkv(mem, w_kv_bf16):
    bsz, n_mem, d_model = mem.shape
    d_mem = w_kv_bf16.shape[1] // 2
    out = jax.ShapeDtypeStruct((bsz, n_mem, d_mem), F32)
    return pl.pallas_call(
        _memkv_kernel,
        grid=(bsz,),
        in_specs=[
            pl.BlockSpec((1, n_mem, d_model), lambda b: (b, 0, 0)),
            pl.BlockSpec((d_model, 2 * d_mem), lambda b: (0, 0)),
        ],
        out_specs=[
            pl.BlockSpec((1, n_mem, d_mem), lambda b: (b, 0, 0)),
            pl.BlockSpec((1, n_mem, d_mem), lambda b: (b, 0, 0)),
        ],
        out_shape=[out, out],
        name="memkv",
    )(mem, w_kv_bf16)


_CONV_PAD = V7X_SUBLANES


def _causal_conv(buf_ref, x3, w_ref, width):
    nb, ts, c = x3.shape
    buf_ref[:, _CONV_PAD:_CONV_PAD + ts, :] = x3
    acc = x3 * w_ref[width - 1:width, :].reshape(1, 1, c)
    for k in range(width - 1):
        start = _CONV_PAD - (width - 1) + k
        acc = acc + buf_ref[:, start:start + ts, :] * w_ref[k:k + 1, :].reshape(1, 1, c)
    tail = buf_ref[:, _CONV_PAD + ts - (width - 1):_CONV_PAD + ts, :]
    buf_ref[:, _CONV_PAD - (width - 1):_CONV_PAD, :] = tail
    return acc, tail


def _mixer_kernel(alpha, n_heads_mem,
                  x_ref, mk_ref, mv_ref, lc0_ref, h0_ref, sc0_ref,
                  w_in_ref, cw_ref, cb_ref, wa_ref, ba_ref, wx_ref, bx_ref, lam_ref,
                  scw_ref, g_lru_ref, g_sc_ref, g_mem_ref, w_out_ref, ln_g_ref, ln_b_ref,
                  y_ref, lc_ref, h_ref, sc_ref,
                  cbuf_a, cbuf_b, a_scr, b_scr, hall, hcar, q_scr, ymem):
    nb, ts, d_model = x_ref.shape
    rows = nb * ts
    d_lru = h_ref.shape[-1]
    d_sc = sc_ref.shape[-1]
    d_mem = mk_ref.shape[-1]
    lru_w = cw_ref.shape[0]
    sc_w = scw_ref.shape[0]
    j = pl.program_id(1)

    @pl.when(j == 0)
    def _():
        cbuf_a[:, _CONV_PAD - (lru_w - 1):_CONV_PAD, :] = lc0_ref[...]
        cbuf_b[:, _CONV_PAD - (sc_w - 1):_CONV_PAD, :] = sc0_ref[...]
        hcar[...] = h0_ref[...]

    x2 = x_ref[...].reshape(rows, d_model)
    proj = jnp.dot(x2.astype(BF16), w_in_ref[...], preferred_element_type=F32)
    o = 0
    xl = proj[:, o:o + d_lru]; o += d_lru
    gate = proj[:, o:o + d_lru]; o += d_lru
    sc_b = proj[:, o:o + d_sc]; o += d_sc
    sc_c = proj[:, o:o + d_sc]; o += d_sc
    sc_x = proj[:, o:o + d_sc]; o += d_sc
    q_scr[...] = proj[:, o:o + d_mem]

    xc3, lc_tail = _causal_conv(cbuf_a, xl.reshape(nb, ts, d_lru), cw_ref, lru_w)
    lc_ref[...] = lc_tail
    xc = xc3.reshape(rows, d_lru) + cb_ref[...]
    xcb = xc.astype(BF16)
    r = jax.nn.sigmoid(jnp.dot(xcb, wa_ref[...], preferred_element_type=F32) + ba_ref[...])
    i = jax.nn.sigmoid(jnp.dot(xcb, wx_ref[...], preferred_element_type=F32) + bx_ref[...])
    lam = lam_ref[...]
    log_sig = -(jnp.maximum(-lam, 0.0) + jnp.log1p(jnp.exp(-jnp.abs(lam))))
    log_a = LRU_C * r * log_sig
    a = jnp.exp(log_a)
    bt = jnp.sqrt(-jnp.tanh(log_a) * (a * a + 1.0)) * (i * xc)
    n_lt = d_lru // V7X_LANES
    a3 = a.reshape(nb, ts, d_lru)
    b3 = bt.reshape(nb, ts, d_lru)
    for c in range(n_lt):
        lt = slice(c * V7X_LANES, (c + 1) * V7X_LANES)
        a_scr[c] = jnp.transpose(a3[:, :, lt], (1, 0, 2))
        b_scr[c] = jnp.transpose(b3[:, :, lt], (1, 0, 2))

    def scan_step(s, hs):
        new = []
        for c in range(n_lt):
            h = a_scr[c, s] * hs[c] + b_scr[c, s]
            hall[c, s] = h
            new.append(h)
        return tuple(new)

    h0 = hcar[...]
    hs = lax.fori_loop(0, ts, scan_step,
                       tuple(h0[:, c * V7X_LANES:(c + 1) * V7X_LANES] for c in range(n_lt)),
                       unroll=8)
    h_last = jnp.concatenate(hs, axis=-1)
    hcar[...] = h_last
    h_ref[...] = h_last
    h_all = jnp.concatenate(
        [jnp.transpose(hall[c], (1, 0, 2)).reshape(rows, V7X_LANES) for c in range(n_lt)], axis=-1)
    y_lru = h_all * _gelu_exact(gate)

    u3, sc_tail = _causal_conv(cbuf_b, (sc_c * sc_x).reshape(nb, ts, d_sc), scw_ref, sc_w)
    sc_ref[...] = sc_tail
    y_sc = sc_b * u3.reshape(rows, d_sc)

    hd = d_mem // n_heads_mem
    head_of_lane = lax.broadcasted_iota(jnp.int32, (ts, d_mem), 1) >> int(math.log2(hd))
    scale = hd ** -0.5

    def attn_step(b, carry):
        rws = pl.ds(pl.multiple_of(b * ts, ts), ts)
        qb = q_scr[rws, :]
        qm = jnp.concatenate(
            [jnp.where(head_of_lane == h, qb, 0.0) for h in range(n_heads_mem)], axis=0)
        kb = mk_ref[b].astype(BF16)
        vb = mv_ref[b].astype(BF16)
        s = lax.dot_general(qm.astype(BF16), kb, _NT, preferred_element_type=F32) * scale
        s = s - jnp.max(s, axis=-1, keepdims=True)
        p = jnp.exp(s)
        p = p / jnp.sum(p, axis=-1, keepdims=True)
        yh = jnp.dot(p.astype(BF16), vb, preferred_element_type=F32)
        yb = jnp.zeros((ts, d_mem), F32)
        for h in range(n_heads_mem):
            yb = yb + jnp.where(head_of_lane == h, yh[h * ts:(h + 1) * ts, :], 0.0)
        ymem[rws, :] = yb
        return carry

    lax.fori_loop(0, nb, attn_step, 0, unroll=2)

    y = jnp.concatenate(
        [_rmsnorm(y_lru, g_lru_ref[...]), _rmsnorm(y_sc, g_sc_ref[...]),
         _rmsnorm(ymem[...], g_mem_ref[...])], axis=-1)
    out = jnp.dot(y.astype(BF16), w_out_ref[...], preferred_element_type=F32)
    y_ref[...] = _layernorm(alpha * x2 + out, ln_g_ref[...], ln_b_ref[...]).reshape(nb, ts, d_model)


def _const_spec(shape):
    nd = len(shape)
    return pl.BlockSpec(shape, lambda i, j: (0,) * nd)


def _mixer(x, mem_k, mem_v, lc0, h0, sc0, wts, *, alpha, n_heads_mem, nb, ts):
    bsz, seq, d_model = x.shape
    n_mem, d_mem = mem_k.shape[1:]
    d_lru = h0.shape[-1]
    d_sc = sc0.shape[-1]
    lru_w = lc0.shape[1] + 1
    sc_w = sc0.shape[1] + 1
    rows = nb * ts
    grid = (bsz // nb, seq // ts)
    weights = [wts[k] for k in ("w_in", "lru_conv_w", "lru_conv_b", "wa", "ba", "wx", "bx", "lam",
                                "sc_conv_w", "g_lru", "g_sc", "g_mem", "w_out", "ln_g", "ln_b")]
    in_specs = [
        pl.BlockSpec((nb, ts, d_model), lambda i, j: (i, j, 0)),
        pl.BlockSpec((nb, n_mem, d_mem), lambda i, j: (i, 0, 0)),
        pl.BlockSpec((nb, n_mem, d_mem), lambda i, j: (i, 0, 0)),
        pl.BlockSpec((nb, lru_w - 1, d_lru), lambda i, j: (i, 0, 0)),
        pl.BlockSpec((nb, d_lru), lambda i, j: (i, 0)),
        pl.BlockSpec((nb, sc_w - 1, d_sc), lambda i, j: (i, 0, 0)),
    ] + [_const_spec(w.shape) for w in weights]
    out_specs = [
        pl.BlockSpec((nb, ts, d_model), lambda i, j: (i, j, 0)),
        pl.BlockSpec((nb, lru_w - 1, d_lru), lambda i, j: (i, 0, 0)),
        pl.BlockSpec((nb, d_lru), lambda i, j: (i, 0)),
        pl.BlockSpec((nb, sc_w - 1, d_sc), lambda i, j: (i, 0, 0)),
    ]
    out_shape = [
        jax.ShapeDtypeStruct((bsz, seq, d_model), F32),
        jax.ShapeDtypeStruct((bsz, lru_w - 1, d_lru), F32),
        jax.ShapeDtypeStruct((bsz, d_lru), F32),
        jax.ShapeDtypeStruct((bsz, sc_w - 1, d_sc), F32),
    ]
    scratch = [
        pltpu.VMEM((nb, ts + _CONV_PAD, d_lru), F32),
        pltpu.VMEM((nb, ts + _CONV_PAD, d_sc), F32),
        pltpu.VMEM((d_lru // V7X_LANES, ts, nb, V7X_LANES), F32),
        pltpu.VMEM((d_lru // V7X_LANES, ts, nb, V7X_LANES), F32),
        pltpu.VMEM((d_lru // V7X_LANES, ts, nb, V7X_LANES), F32),
        pltpu.VMEM((nb, d_lru), F32),
        pltpu.VMEM((rows, d_mem), F32),
        pltpu.VMEM((rows, d_mem), F32),
    ]
    return pl.pallas_call(
        functools.partial(_mixer_kernel, alpha, n_heads_mem),
        grid=grid,
        in_specs=in_specs,
        out_specs=out_specs,
        out_shape=out_shape,
        scratch_shapes=scratch,
        compiler_params=pltpu.CompilerParams(
            dimension_semantics=("arbitrary", "arbitrary"),
            vmem_limit_bytes=V7X_VMEM_LIMIT_BYTES),
        name="mixer",
    )(x, mem_k, mem_v, lc0, h0, sc0, *weights)


def _argmax_tree(nodes):
    nodes = list(nodes)
    while len(nodes) > 1:
        nxt = []
        for i in range(0, len(nodes) - 1, 2):
            (av, ak), (bv, bk) = nodes[i], nodes[i + 1]
            nxt.append((jnp.maximum(av, bv), jnp.where(av >= bv, ak, bk)))
        if len(nodes) % 2:
            nxt.append(nodes[-1])
        nodes = nxt
    return nodes[0]


def _pair_list():
    k = PEER_TOPK
    return [(a, b) for a in range(k) for b in range(k // (a + 1))]


def _topk_kernel(n_keys, n_heads, x_ref, wq_ref, kb_ref, g_ref, c_ref, j_ref,
                 q_scr, s_scr, v_scr, i_scr, cv_scr, ck_scr, ts_scr, tk_scr):
    tm = x_ref.shape[0]
    lanes = V7X_LANES
    k = PEER_TOPK
    n_experts = n_keys * n_keys
    dk2 = kb_ref.shape[3]
    pairs = _pair_list()
    q_scr[...] = jnp.dot(x_ref[...].astype(BF16), wq_ref[...],
                         preferred_element_type=F32).astype(BF16)
    none = jnp.full((n_heads, lanes), -1.0, F32)

    def block_body(blk, carry):
        rows = pl.ds(pl.multiple_of(blk * lanes, lanes), lanes)
        q = q_scr[rows, :]
        for p in range(2):
            per_head = []
            for h in range(n_heads):
                col = (p * n_heads + h) * dk2
                per_head.append(lax.dot_general(kb_ref[p, h], q[:, col:col + dk2], _NT,
                                                preferred_element_type=F32))
            s_scr[p] = jnp.transpose(jnp.stack(per_head), (1, 0, 2))

        def rank_body(it, prev):
            new = []
            for p in range(2):
                leaves = []
                for key in range(n_keys):
                    v = jnp.where(prev[p] == float(key), -jnp.inf, s_scr[p, key])
                    s_scr[p, key] = v
                    leaves.append((v, float(key)))
                m, idx = _argmax_tree(leaves)
                v_scr[p, it] = m
                i_scr[p, it] = idx
                new.append(idx)
            return tuple(new)

        lax.fori_loop(0, k, rank_body, (none, none))

        for n, (a, b) in enumerate(pairs):
            cv_scr[n] = v_scr[0, a] + v_scr[1, b]
            ck_scr[n] = float((a * k + b) * n_experts) + (i_scr[0, a] * float(n_keys) + i_scr[1, b])

        def pair_body(it, prev):
            leaves = []
            for n in range(len(pairs)):
                kc = ck_scr[n]
                v = jnp.where(kc == prev, -jnp.inf, cv_scr[n])
                cv_scr[n] = v
                leaves.append((v, kc))
            m, kk = _argmax_tree(leaves)
            ts_scr[it] = m
            tk_scr[it] = kk
            return kk

        lax.fori_loop(0, k, pair_body, none)

        top = ts_scr[...]
        ex = jnp.exp(top - top[0:1])
        gates = ex / jnp.sum(ex, axis=0, keepdims=True)
        e = tk_scr[...].astype(jnp.int32) & (n_experts - 1)
        g_ref[rows, :] = gates.reshape(k * n_heads, lanes).T
        c_ref[rows, :] = (e >> int(math.log2(n_keys))).astype(F32).reshape(k * n_heads, lanes).T
        j_ref[rows, :] = (e & (n_keys - 1)).astype(F32).reshape(k * n_heads, lanes).T
        return carry

    lax.fori_loop(0, tm // lanes, block_body, 0)


def _peer_topk(x1, wq_perm, keys_t, *, n_heads, tm):
    t, d_model = x1.shape
    n_keys = keys_t.shape[2]
    hk = n_heads * PEER_TOPK
    n_pairs = len(_pair_list())
    out = jax.ShapeDtypeStruct((t, hk), F32)
    vreg = (n_heads, V7X_LANES)
    return pl.pallas_call(
        functools.partial(_topk_kernel, n_keys, n_heads),
        grid=(t // tm,),
        in_specs=[
            pl.BlockSpec((tm, d_model), lambda i: (i, 0)),
            pl.BlockSpec(wq_perm.shape, lambda i: (0, 0)),
            pl.BlockSpec(keys_t.shape, lambda i: (0, 0, 0, 0)),
        ],
        out_specs=[pl.BlockSpec((tm, hk), lambda i: (i, 0))] * 3,
        out_shape=[out, out, out],
        scratch_shapes=[
            pltpu.VMEM((tm, wq_perm.shape[1]), BF16),
            pltpu.VMEM((2, n_keys) + vreg, F32),
            pltpu.VMEM((2, PEER_TOPK) + vreg, F32),
            pltpu.VMEM((2, PEER_TOPK) + vreg, F32),
            pltpu.VMEM((n_pairs,) + vreg, F32),
            pltpu.VMEM((n_pairs,) + vreg, F32),
            pltpu.VMEM((PEER_TOPK,) + vreg, F32),
            pltpu.VMEM((PEER_TOPK,) + vreg, F32),
        ],
        compiler_params=pltpu.CompilerParams(
            dimension_semantics=("arbitrary",),
            vmem_limit_bytes=V7X_VMEM_LIMIT_BYTES),
        name="peer_topk",
    )(x1, wq_perm, keys_t)


_GATE_GROUP = 2 * V7X_SUBLANES


def _experts_kernel(alpha, n_keys, x_ref, g_ref, c_ref, j_ref, ut_ref, v_ref, ln_g_ref, ln_b_ref,
                    o_ref, gate_scr, acc, xb_scr):
    tm, d_model = x_ref.shape
    ec = ut_ref.shape[1]
    chunks_c = ec // n_keys
    kstep = pl.program_id(1)

    @pl.when(kstep == 0)
    def _():
        xb_scr[...] = x_ref[...].astype(BF16)
        acc[...] = jnp.zeros_like(acc)
        row_iota = lax.broadcasted_iota(jnp.int32, (n_keys, g_ref.shape[1]), 0).astype(F32)

        def group_body(tg, carry):
            base = pl.multiple_of(tg * _GATE_GROUP, _GATE_GROUP)
            per_token = []
            for tl in range(_GATE_GROUP):
                row = pl.ds(base + tl, 1)
                ct = jnp.where(c_ref[row, :] == row_iota, g_ref[row, :], 0.0).astype(BF16)
                jt = jnp.where(j_ref[row, :] == row_iota, 1.0, 0.0).astype(BF16)
                per_token.append(lax.dot_general(ct, jt, _NT, preferred_element_type=F32).astype(BF16))
            gate_scr[:, pl.ds(base, _GATE_GROUP), :] = jnp.transpose(jnp.stack(per_token), (1, 0, 2))
            return carry

        lax.fori_loop(0, tm // _GATE_GROUP, group_body, 0, unroll=4)

    a = jnp.dot(xb_scr[...], ut_ref[...], preferred_element_type=F32)
    gate = jnp.concatenate([gate_scr[kstep * chunks_c + i] for i in range(chunks_c)], axis=-1)
    w = _gelu_exact(a).astype(BF16) * gate
    acc[...] += jnp.dot(w, v_ref[...], preferred_element_type=F32)

    @pl.when(kstep == pl.num_programs(1) - 1)
    def _():
        o_ref[...] = _layernorm(alpha * x_ref[...] + acc[...], ln_g_ref[...], ln_b_ref[...])


def _peer_experts(x1, gates, cidx, jidx, ut_bf16, v_bf16, ln_g, ln_b, *, alpha, n_keys, tm, ec):
    t, d_model = x1.shape
    n_exp = v_bf16.shape[0]
    hk = gates.shape[1]
    n_chunks = n_exp // ec
    return pl.pallas_call(
        functools.partial(_experts_kernel, alpha, n_keys),
        grid=(t // tm, n_chunks),
        in_specs=[
            pl.BlockSpec((tm, d_model), lambda i, k: (i, 0)),
            pl.BlockSpec((tm, hk), lambda i, k: (i, 0)),
            pl.BlockSpec((tm, hk), lambda i, k: (i, 0)),
            pl.BlockSpec((tm, hk), lambda i, k: (i, 0)),
            pl.BlockSpec((d_model, ec), lambda i, k: (0, k)),
            pl.BlockSpec((ec, d_model), lambda i, k: (k, 0)),
            pl.BlockSpec((1, d_model), lambda i, k: (0, 0)),
            pl.BlockSpec((1, d_model), lambda i, k: (0, 0)),
        ],
        out_specs=pl.BlockSpec((tm, d_model), lambda i, k: (i, 0)),
        out_shape=jax.ShapeDtypeStruct((t, d_model), F32),
        scratch_shapes=[
            pltpu.VMEM((n_keys, tm, n_keys), BF16),
            pltpu.VMEM((tm, d_model), F32),
            pltpu.VMEM((tm, d_model), BF16),
        ],
        compiler_params=pltpu.CompilerParams(
            dimension_semantics=("arbitrary", "arbitrary"),
            vmem_limit_bytes=V7X_VMEM_LIMIT_BYTES),
        name="peer_experts",
    )(x1, gates, cidx, jidx, ut_bf16, v_bf16, ln_g, ln_b)


def _largest_tile(n, cap, mult):
    best = None
    for c in range(mult, min(n, cap) + 1, mult):
        if n % c == 0:
            best = c
    assert best is not None, (n, cap, mult)
    return best


def _block_diag(w):
    h, a, b = w.shape
    return jnp.einsum("hij,hg->higj", w, jnp.eye(h, dtype=w.dtype)).reshape(h * a, h * b)


def kernel(x_prompt, x_sample, mem_prompt, cache_mem_k, cache_mem_v, state_lru_conv, state_lru_h, state_sc_conv, w_in, lru_conv_w, lru_conv_b, w_rg_a, b_rg_a, w_rg_x, b_rg_x, lru_lambda, sc_conv_w, w_mem_kv, g_lru, g_sc, g_mem, w_out, ln1_g, ln1_b, peer_wq, peer_keys, peer_u, peer_v, ln2_g, ln2_b):
    depth = w_in.shape[0]
    alpha = (2.0 * depth) ** 0.25
    bsz, seq, d_model = x_prompt.shape
    dbsz, dseq, _ = x_sample.shape
    n_mem, mem_heads, mem_hd = cache_mem_k.shape[2:]
    d_mem = mem_heads * mem_hd
    d_lru = state_lru_h.shape[-1]
    d_sc = state_sc_conv.shape[-1]
    n_heads, _, n_keys, dk2 = peer_keys.shape[1:]
    assert n_keys == V7X_LANES and dk2 == V7X_LANES

    xp, xs = x_prompt, x_sample
    outs = {k: [] for k in ("mk", "mv", "lc_p", "lh_p", "sc_p", "lc_s", "lh_s", "sc_s")}
    row = lambda v: v.reshape(1, -1)
    for l in range(depth):
        wts = dict(
            w_in=w_in[l].astype(BF16), lru_conv_w=lru_conv_w[l], lru_conv_b=row(lru_conv_b[l]),
            wa=_block_diag(w_rg_a[l]).astype(BF16), ba=row(b_rg_a[l]),
            wx=_block_diag(w_rg_x[l]).astype(BF16), bx=row(b_rg_x[l]), lam=row(lru_lambda[l]),
            sc_conv_w=sc_conv_w[l], g_lru=row(g_lru[l]), g_sc=row(g_sc[l]), g_mem=row(g_mem[l]),
            w_out=w_out[l].astype(BF16), ln_g=row(ln1_g[l]), ln_b=row(ln1_b[l]))
        mk, mv = _memkv(mem_prompt, w_mem_kv[l].astype(BF16))
        mix = functools.partial(_mixer, wts=wts, alpha=alpha, n_heads_mem=mem_heads)
        x1p, lc_p, lh_p, sc_p = mix(
            xp, mk, mv,
            jnp.zeros((bsz,) + state_lru_conv.shape[2:], F32), jnp.zeros((bsz, d_lru), F32),
            jnp.zeros((bsz,) + state_sc_conv.shape[2:], F32),
            nb=bsz, ts=_largest_tile(seq, 128, V7X_SUBLANES))
        x1s, lc_s, lh_s, sc_s = mix(
            xs, cache_mem_k[l].reshape(dbsz, n_mem, d_mem), cache_mem_v[l].reshape(dbsz, n_mem, d_mem),
            state_lru_conv[l], state_lru_h[l], state_sc_conv[l],
            nb=_largest_tile(dbsz, 16, V7X_SUBLANES), ts=dseq)

        wq_perm = peer_wq[l].astype(BF16).reshape(d_model, n_heads, 2, dk2).transpose(0, 2, 1, 3)
        wq_perm = wq_perm.reshape(d_model, 2 * n_heads * dk2)
        keys_t = peer_keys[l].astype(BF16).transpose(1, 0, 2, 3)
        ut_bf16 = peer_u[l].T.astype(BF16)
        v_bf16 = peer_v[l].astype(BF16)

        def peer(x1_3d):
            x1 = x1_3d.reshape(-1, d_model)
            t = x1.shape[0]
            gates, cidx, jidx = _peer_topk(x1, wq_perm, keys_t, n_heads=n_heads,
                                           tm=_largest_tile(t, 512, V7X_LANES))
            x2 = _peer_experts(x1, gates, cidx, jidx, ut_bf16, v_bf16, row(ln2_g[l]), row(ln2_b[l]),
                               alpha=alpha, n_keys=n_keys, tm=_largest_tile(t, 512, _GATE_GROUP),
                               ec=8 * n_keys)
            return x2.reshape(x1_3d.shape)

        xp = peer(x1p)
        xs = peer(x1s)

        outs["mk"].append(mk.reshape(bsz, n_mem, mem_heads, mem_hd))
        outs["mv"].append(mv.reshape(bsz, n_mem, mem_heads, mem_hd))
        for name, val in (("lc_p", lc_p), ("lh_p", lh_p), ("sc_p", sc_p),
                          ("lc_s", lc_s), ("lh_s", lh_s), ("sc_s", sc_s)):
            outs[name].append(val)
    st = {k: jnp.stack(v) for k, v in outs.items()}
    return (xp, xs, st["mk"], st["mv"], st["lc_p"], st["lh_p"], st["sc_p"],
            st["lc_s"], st["lh_s"], st["sc_s"])
```

```python
import functools
import math

import jax
import jax.numpy as jnp
from jax import lax
from jax.experimental import pallas as pl
from jax.experimental.pallas import tpu as pltpu

F32 = jnp.float32
BF16 = jnp.bfloat16

LRU_C = 8.0
LN_EPS = 1e-5
RMS_EPS = 1e-6
PEER_TOPK = 16

V7X_LANES = 128
V7X_SUBLANES = 8
V7X_VMEM_LIMIT_BYTES = 56 * 1024 * 1024

_NT = (((1,), (1,)), ((), ()))


def _gelu_exact(x):
    return 0.5 * x * (1.0 + lax.erf(x * (1.0 / math.sqrt(2.0))))


def _layernorm(x, g, b):
    mu = jnp.mean(x, axis=-1, keepdims=True)
    xc = x - mu
    var = jnp.mean(xc * xc, axis=-1, keepdims=True)
    return xc * lax.rsqrt(var + LN_EPS) * g + b


def _rmsnorm(x, g):
    return x * lax.rsqrt(jnp.mean(x * x, axis=-1, keepdims=True) + RMS_EPS) * g


def _memkv_kernel(mem_ref, w_ref, k_ref, v_ref):
    d_mem = k_ref.shape[-1]
    kv = jnp.dot(mem_ref[0].astype(BF16), w_ref[...], preferred_element_type=F32)
    k_ref[0] = kv[:, :d_mem]
    v_ref[0] = kv[:, d_mem:]


def _memkv(mem, w_kv_bf16):
    bsz, n_mem, d_model = mem.shape
    d_mem = w_kv_bf16.shape[1] // 2
    out = jax.ShapeDtypeStruct((bsz, n_mem, d_mem), F32)
    return pl.pallas_call(
        _memkv_kernel,
        grid=(bsz,),
        in_specs=[
            pl.BlockSpec((1, n_mem, d_model), lambda b: (b, 0, 0)),
            pl.BlockSpec((d_model, 2 * d_mem), lambda b: (0, 0)),
        ],
        out_specs=[
            pl.BlockSpec((1, n_mem, d_mem), lambda b: (b, 0, 0)),
            pl.BlockSpec((1, n_mem, d_mem), lambda b: (b, 0, 0)),
        ],
        out_shape=[out, out],
        name="memkv",
    )(mem, w_kv_bf16)


_CONV_PAD = V7X_SUBLANES


def _causal_conv(buf_ref, x3, w_ref, width):
    nb, ts, c = x3.shape
    buf_ref[:, _CONV_PAD:_CONV_PAD + ts, :] = x3
    acc = x3 * w_ref[width - 1:width, :].reshape(1, 1, c)
    for k in range(width - 1):
        start = _CONV_PAD - (width - 1) + k
        acc = acc + buf_ref[:, start:start + ts, :] * w_ref[k:k + 1, :].reshape(1, 1, c)
    tail = buf_ref[:, _CONV_PAD + ts - (width - 1):_CONV_PAD + ts, :]
    buf_ref[:, _CONV_PAD - (width - 1):_CONV_PAD, :] = tail
    return acc, tail


def _mixer_kernel(alpha, n_heads_mem,
                  x_ref, mk_ref, mv_ref, lc0_ref, h0_ref, sc0_ref,
                  w_in_ref, cw_ref, cb_ref, wa_ref, ba_ref, wx_ref, bx_ref, lam_ref,
                  scw_ref, g_lru_ref, g_sc_ref, g_mem_ref, w_out_ref, ln_g_ref, ln_b_ref,
                  y_ref, lc_ref, h_ref, sc_ref,
                  cbuf_a, cbuf_b, a_scr, b_scr, hall, hcar, q_scr, ymem):
    nb, ts, d_model = x_ref.shape
    rows = nb * ts
    d_lru = h_ref.shape[-1]
    d_sc = sc_ref.shape[-1]
    d_mem = mk_ref.shape[-1]
    lru_w = cw_ref.shape[0]
    sc_w = scw_ref.shape[0]
    j = pl.program_id(1)

    @pl.when(j == 0)
    def _():
        cbuf_a[:, _CONV_PAD - (lru_w - 1):_CONV_PAD, :] = lc0_ref[...]
        cbuf_b[:, _CONV_PAD - (sc_w - 1):_CONV_PAD, :] = sc0_ref[...]
        hcar[...] = h0_ref[...]

    x2 = x_ref[...].reshape(rows, d_model)
    proj = jnp.dot(x2.astype(BF16), w_in_ref[...], preferred_element_type=F32)
    o = 0
    xl = proj[:, o:o + d_lru]; o += d_lru
    gate = proj[:, o:o + d_lru]; o += d_lru
    sc_b = proj[:, o:o + d_sc]; o += d_sc
    sc_c = proj[:, o:o + d_sc]; o += d_sc
    sc_x = proj[:, o:o + d_sc]; o += d_sc
    q_scr[...] = proj[:, o:o + d_mem]

    xc3, lc_tail = _causal_conv(cbuf_a, xl.reshape(nb, ts, d_lru), cw_ref, lru_w)
    lc_ref[...] = lc_tail
    xc = xc3.reshape(rows, d_lru) + cb_ref[...]
    xcb = xc.astype(BF16)
    r = jax.nn.sigmoid(jnp.dot(xcb, wa_ref[...], preferred_element_type=F32) + ba_ref[...])
    i = jax.nn.sigmoid(jnp.dot(xcb, wx_ref[...], preferred_element_type=F32) + bx_ref[...])
    lam = lam_ref[...]
    log_sig = -(jnp.maximum(-lam, 0.0) + jnp.log1p(jnp.exp(-jnp.abs(lam))))
    log_a = LRU_C * r * log_sig
    a = jnp.exp(log_a)
    bt = jnp.sqrt(-jnp.tanh(log_a) * (a * a + 1.0)) * (i * xc)
    n_lt = d_lru // V7X_LANES
    a3 = a.reshape(nb, ts, d_lru)
    b3 = bt.reshape(nb, ts, d_lru)
    for c in range(n_lt):
        lt = slice(c * V7X_LANES, (c + 1) * V7X_LANES)
        a_scr[c] = jnp.transpose(a3[:, :, lt], (1, 0, 2))
        b_scr[c] = jnp.transpose(b3[:, :, lt], (1, 0, 2))

    def scan_step(s, hs):
        new = []
        for c in range(n_lt):
            h = a_scr[c, s] * hs[c] + b_scr[c, s]
            hall[c, s] = h
            new.append(h)
        return tuple(new)

    h0 = hcar[...]
    hs = lax.fori_loop(0, ts, scan_step,
                       tuple(h0[:, c * V7X_LANES:(c + 1) * V7X_LANES] for c in range(n_lt)),
                       unroll=8)
    h_last = jnp.concatenate(hs, axis=-1)
    hcar[...] = h_last
    h_ref[...] = h_last
    h_all = jnp.concatenate(
        [jnp.transpose(hall[c], (1, 0, 2)).reshape(rows, V7X_LANES) for c in range(n_lt)], axis=-1)
    y_lru = h_all * _gelu_exact(gate)

    u3, sc_tail = _causal_conv(cbuf_b, (sc_c * sc_x).reshape(nb, ts, d_sc), scw_ref, sc_w)
    sc_ref[...] = sc_tail
    y_sc = sc_b * u3.reshape(rows, d_sc)

    hd = d_mem // n_heads_mem
    head_of_lane = lax.broadcasted_iota(jnp.int32, (ts, d_mem), 1) >> int(math.log2(hd))
    scale = hd ** -0.5

    def attn_step(b, carry):
        rws = pl.ds(pl.multiple_of(b * ts, ts), ts)
        qb = q_scr[rws, :]
        qm = jnp.concatenate(
            [jnp.where(head_of_lane == h, qb, 0.0) for h in range(n_heads_mem)], axis=0)
        kb = mk_ref[b].astype(BF16)
        vb = mv_ref[b].astype(BF16)
        s = lax.dot_general(qm.astype(BF16), kb, _NT, preferred_element_type=F32) * scale
        s = s - jnp.max(s, axis=-1, keepdims=True)
        p = jnp.exp(s)
        p = p / jnp.sum(p, axis=-1, keepdims=True)
        yh = jnp.dot(p.astype(BF16), vb, preferred_element_type=F32)
        yb = jnp.zeros((ts, d_mem), F32)
        for h in range(n_heads_mem):
            yb = yb + jnp.where(head_of_lane == h, yh[h * ts:(h + 1) * ts, :], 0.0)
        ymem[rws, :] = yb
        return carry

    lax.fori_loop(0, nb, attn_step, 0, unroll=2)

    y = jnp.concatenate(
        [_rmsnorm(y_lru, g_lru_ref[...]), _rmsnorm(y_sc, g_sc_ref[...]),
         _rmsnorm(ymem[...], g_mem_ref[...])], axis=-1)
    out = jnp.dot(y.astype(BF16), w_out_ref[...], preferred_element_type=F32)
    y_ref[...] = _layernorm(alpha * x2 + out, ln_g_ref[...], ln_b_ref[...]).reshape(nb, ts, d_model)


def _const_spec(shape):
    nd = len(shape)
    return pl.BlockSpec(shape, lambda i, j: (0,) * nd)


def _mixer(x, mem_k, mem_v, lc0, h0, sc0, wts, *, alpha, n_heads_mem, nb, ts):
    bsz, seq, d_model = x.shape
    n_mem, d_mem = mem_k.shape[1:]
    d_lru = h0.shape[-1]
    d_sc = sc0.shape[-1]
    lru_w = lc0.shape[1] + 1
    sc_w = sc0.shape[1] + 1
    rows = nb * ts
    grid = (bsz // nb, seq // ts)
    weights = [wts[k] for k in ("w_in", "lru_conv_w", "lru_conv_b", "wa", "ba", "wx", "bx", "lam",
                                "sc_conv_w", "g_lru", "g_sc", "g_mem", "w_out", "ln_g", "ln_b")]
    in_specs = [
        pl.BlockSpec((nb, ts, d_model), lambda i, j: (i, j, 0)),
        pl.BlockSpec((nb, n_mem, d_mem), lambda i, j: (i, 0, 0)),
        pl.BlockSpec((nb, n_mem, d_mem), lambda i, j: (i, 0, 0)),
        pl.BlockSpec((nb, lru_w - 1, d_lru), lambda i, j: (i, 0, 0)),
        pl.BlockSpec((nb, d_lru), lambda i, j: (i, 0)),
        pl.BlockSpec((nb, sc_w - 1, d_sc), lambda i, j: (i, 0, 0)),
    ] + [_const_spec(w.shape) for w in weights]
    out_specs = [
        pl.BlockSpec((nb, ts, d_model), lambda i, j: (i, j, 0)),
        pl.BlockSpec((nb, lru_w - 1, d_lru), lambda i, j: (i, 0, 0)),
        pl.BlockSpec((nb, d_lru), lambda i, j: (i, 0)),
        pl.BlockSpec((nb, sc_w - 1, d_sc), lambda i, j: (i, 0, 0)),
    ]
    out_shape = [
        jax.ShapeDtypeStruct((bsz, seq, d_model), F32),
        jax.ShapeDtypeStruct((bsz, lru_w - 1, d_lru), F32),
        jax.ShapeDtypeStruct((bsz, d_lru), F32),
        jax.ShapeDtypeStruct((bsz, sc_w - 1, d_sc), F32),
    ]
    scratch = [
        pltpu.VMEM((nb, ts + _CONV_PAD, d_lru), F32),
        pltpu.VMEM((nb, ts + _CONV_PAD, d_sc), F32),
        pltpu.VMEM((d_lru // V7X_LANES, ts, nb, V7X_LANES), F32),
        pltpu.VMEM((d_lru // V7X_LANES, ts, nb, V7X_LANES), F32),
        pltpu.VMEM((d_lru // V7X_LANES, ts, nb, V7X_LANES), F32),
        pltpu.VMEM((nb, d_lru), F32),
        pltpu.VMEM((rows, d_mem), F32),
        pltpu.VMEM((rows, d_mem), F32),
    ]
    return pl.pallas_call(
        functools.partial(_mixer_kernel, alpha, n_heads_mem),
        grid=grid,
        in_specs=in_specs,
        out_specs=out_specs,
        out_shape=out_shape,
        scratch_shapes=scratch,
        compiler_params=pltpu.CompilerParams(
            dimension_semantics=("arbitrary", "arbitrary"),
            vmem_limit_bytes=V7X_VMEM_LIMIT_BYTES),
        name="mixer",
    )(x, mem_k, mem_v, lc0, h0, sc0, *weights)


def _argmax_tree(nodes):
    nodes = list(nodes)
    while len(nodes) > 1:
        nxt = []
        for i in range(0, len(nodes) - 1, 2):
            (av, ak), (bv, bk) = nodes[i], nodes[i + 1]
            nxt.append((jnp.maximum(av, bv), jnp.where(av >= bv, ak, bk)))
        if len(nodes) % 2:
            nxt.append(nodes[-1])
        nodes = nxt
    return nodes[0]


def _pair_list():
    k = PEER_TOPK
    return [(a, b) for a in range(k) for b in range(k // (a + 1))]


def _sort_network(n):
    pairs = []

    def merge(lo, hi, r):
        step = r * 2
        if step < hi - lo:
            merge(lo, hi, step)
            merge(lo + r, hi, step)
            pairs.extend((i, i + r) for i in range(lo + r, hi - r, step))
        else:
            pairs.append((lo, lo + r))

    def sort(lo, hi):
        if hi - lo >= 1:
            mid = lo + (hi - lo) // 2
            sort(lo, mid)
            sort(mid + 1, hi)
            merge(lo, hi, 1)

    sort(0, n - 1)
    return tuple(pairs)


def _compare_exchange(v, k, i, j):
    ge = v[i] >= v[j]
    v[i], v[j] = jnp.maximum(v[i], v[j]), jnp.minimum(v[i], v[j])
    k[i], k[j] = jnp.where(ge, k[i], k[j]), jnp.where(ge, k[j], k[i])


def _sorted_top16(load, n_keys):
    k = PEER_TOPK
    net = _sort_network(k)
    groups = []
    for g0 in range(0, n_keys, k):
        v = [load(g0 + i) for i in range(k)]
        kk = [float(g0 + i) for i in range(k)]
        for i, j in net:
            _compare_exchange(v, kk, i, j)
        groups.append((v, kk))
    tv, tk = groups[0]
    dropped = None
    for bv, bk in groups[1:]:
        cv, ck = [], []
        for i in range(k):
            a, b = tv[i], bv[k - 1 - i]
            cv.append(jnp.maximum(a, b))
            ck.append(jnp.where(a >= b, tk[i], bk[k - 1 - i]))
            lo = jnp.minimum(a, b)
            dropped = lo if dropped is None else jnp.maximum(dropped, lo)
        d = k // 2
        while d >= 1:
            for i in range(k):
                if not i & d:
                    _compare_exchange(cv, ck, i, i + d)
            d //= 2
        tv, tk = cv, ck
    for start in (0, 1):
        for i in range(start, k - 1, 2):
            swap = (tv[i] == tv[i + 1]) & (tk[i] > tk[i + 1])
            tk[i], tk[i + 1] = jnp.where(swap, tk[i + 1], tk[i]), jnp.where(swap, tk[i], tk[i + 1])
    suspect = jnp.where(tv[k - 1] <= dropped, 1.0, 0.0)
    for i in range(k - 2):
        suspect = jnp.maximum(suspect, jnp.where(tv[i] == tv[i + 2], 1.0, 0.0))
    return tv, tk, suspect


def _topk_kernel(n_keys, n_heads, x_ref, wq_ref, kb_ref, g_ref, c_ref, j_ref,
                 q_scr, s_scr, v_scr, i_scr, cv_scr, ck_scr, ts_scr, tk_scr):
    tm = x_ref.shape[0]
    lanes = V7X_LANES
    k = PEER_TOPK
    n_experts = n_keys * n_keys
    dk2 = kb_ref.shape[3]
    pairs = _pair_list()
    q_scr[...] = jnp.dot(x_ref[...].astype(BF16), wq_ref[...],
                         preferred_element_type=F32).astype(BF16)
    none = jnp.full((n_heads, lanes), -1.0, F32)

    def block_body(blk, carry):
        rows = pl.ds(pl.multiple_of(blk * lanes, lanes), lanes)
        q = q_scr[rows, :]
        for p in range(2):
            per_head = []
            for h in range(n_heads):
                col = (p * n_heads + h) * dk2
                per_head.append(lax.dot_general(kb_ref[p, h], q[:, col:col + dk2], _NT,
                                                preferred_element_type=F32))
            s_scr[p] = jnp.transpose(jnp.stack(per_head), (1, 0, 2))

        suspect = None
        for p in range(2):
            tv, tk, sus = _sorted_top16(lambda key, p=p: s_scr[p, key], n_keys)
            for r in range(k):
                v_scr[p, r] = tv[r]
                i_scr[p, r] = tk[r]
            suspect = sus if suspect is None else jnp.maximum(suspect, sus)

        @pl.when(jnp.max(suspect) > 0.0)
        def _():
            def rank_body(it, prev):
                new = []
                for p in range(2):
                    leaves = []
                    for key in range(n_keys):
                        v = jnp.where(prev[p] == float(key), -jnp.inf, s_scr[p, key])
                        s_scr[p, key] = v
                        leaves.append((v, float(key)))
                    m, idx = _argmax_tree(leaves)
                    v_scr[p, it] = m
                    i_scr[p, it] = idx
                    new.append(idx)
                return tuple(new)

            lax.fori_loop(0, k, rank_body, (none, none))

        for n, (a, b) in enumerate(pairs):
            cv_scr[n] = v_scr[0, a] + v_scr[1, b]
            ck_scr[n] = float((a * k + b) * n_experts) + (i_scr[0, a] * float(n_keys) + i_scr[1, b])

        def pair_body(it, prev):
            leaves = []
            for n in range(len(pairs)):
                kc = ck_scr[n]
                v = jnp.where(kc == prev, -jnp.inf, cv_scr[n])
                cv_scr[n] = v
                leaves.append((v, kc))
            m, kk = _argmax_tree(leaves)
            ts_scr[it] = m
            tk_scr[it] = kk
            return kk

        lax.fori_loop(0, k, pair_body, none)

        top = ts_scr[...]
        ex = jnp.exp(top - top[0:1])
        gates = ex / jnp.sum(ex, axis=0, keepdims=True)
        e = tk_scr[...].astype(jnp.int32) & (n_experts - 1)
        g_ref[rows, :] = gates.reshape(k * n_heads, lanes).T
        c_ref[rows, :] = (e >> int(math.log2(n_keys))).astype(F32).reshape(k * n_heads, lanes).T
        j_ref[rows, :] = (e & (n_keys - 1)).astype(F32).reshape(k * n_heads, lanes).T
        return carry

    lax.fori_loop(0, tm // lanes, block_body, 0)


def _peer_topk(x1, wq_perm, keys_t, *, n_heads, tm):
    t, d_model = x1.shape
    n_keys = keys_t.shape[2]
    hk = n_heads * PEER_TOPK
    n_pairs = len(_pair_list())
    out = jax.ShapeDtypeStruct((t, hk), F32)
    vreg = (n_heads, V7X_LANES)
    return pl.pallas_call(
        functools.partial(_topk_kernel, n_keys, n_heads),
        grid=(t // tm,),
        in_specs=[
            pl.BlockSpec((tm, d_model), lambda i: (i, 0)),
            pl.BlockSpec(wq_perm.shape, lambda i: (0, 0)),
            pl.BlockSpec(keys_t.shape, lambda i: (0, 0, 0, 0)),
        ],
        out_specs=[pl.BlockSpec((tm, hk), lambda i: (i, 0))] * 3,
        out_shape=[out, out, out],
        scratch_shapes=[
            pltpu.VMEM((tm, wq_perm.shape[1]), BF16),
            pltpu.VMEM((2, n_keys) + vreg, F32),
            pltpu.VMEM((2, PEER_TOPK) + vreg, F32),
            pltpu.VMEM((2, PEER_TOPK) + vreg, F32),
            pltpu.VMEM((n_pairs,) + vreg, F32),
            pltpu.VMEM((n_pairs,) + vreg, F32),
            pltpu.VMEM((PEER_TOPK,) + vreg, F32),
            pltpu.VMEM((PEER_TOPK,) + vreg, F32),
        ],
        compiler_params=pltpu.CompilerParams(
            dimension_semantics=("arbitrary",),
            vmem_limit_bytes=V7X_VMEM_LIMIT_BYTES),
        name="peer_topk",
    )(x1, wq_perm, keys_t)


_GATE_GROUP = 2 * V7X_SUBLANES


def _experts_kernel(alpha, n_keys, x_ref, g_ref, c_ref, j_ref, ut_ref, v_ref, ln_g_ref, ln_b_ref,
                    o_ref, gate_scr, acc, xb_scr):
    tm, d_model = x_ref.shape
    ec = ut_ref.shape[1]
    chunks_c = ec // n_keys
    kstep = pl.program_id(1)

    @pl.when(kstep == 0)
    def _():
        xb_scr[...] = x_ref[...].astype(BF16)
        acc[...] = jnp.zeros_like(acc)
        row_iota = lax.broadcasted_iota(jnp.int32, (n_keys, g_ref.shape[1]), 0).astype(F32)

        def group_body(tg, carry):
            base = pl.multiple_of(tg * _GATE_GROUP, _GATE_GROUP)
            per_token = []
            for tl in range(_GATE_GROUP):
                row = pl.ds(base + tl, 1)
                ct = jnp.where(c_ref[row, :] == row_iota, g_ref[row, :], 0.0).astype(BF16)
                jt = jnp.where(j_ref[row, :] == row_iota, 1.0, 0.0).astype(BF16)
                per_token.append(lax.dot_general(ct, jt, _NT, preferred_element_type=F32).astype(BF16))
            gate_scr[:, pl.ds(base, _GATE_GROUP), :] = jnp.transpose(jnp.stack(per_token), (1, 0, 2))
            return carry

        lax.fori_loop(0, tm // _GATE_GROUP, group_body, 0, unroll=4)

    a = jnp.dot(xb_scr[...], ut_ref[...], preferred_element_type=F32)
    gate = jnp.concatenate([gate_scr[kstep * chunks_c + i] for i in range(chunks_c)], axis=-1)
    w = _gelu_exact(a).astype(BF16) * gate
    acc[...] += jnp.dot(w, v_ref[...], preferred_element_type=F32)

    @pl.when(kstep == pl.num_programs(1) - 1)
    def _():
        o_ref[...] = _layernorm(alpha * x_ref[...] + acc[...], ln_g_ref[...], ln_b_ref[...])


def _peer_experts(x1, gates, cidx, jidx, ut_bf16, v_bf16, ln_g, ln_b, *, alpha, n_keys, tm, ec):
    t, d_model = x1.shape
    n_exp = v_bf16.shape[0]
    hk = gates.shape[1]
    n_chunks = n_exp // ec
    return pl.pallas_call(
        functools.partial(_experts_kernel, alpha, n_keys),
        grid=(t // tm, n_chunks),
        in_specs=[
            pl.BlockSpec((tm, d_model), lambda i, k: (i, 0)),
            pl.BlockSpec((tm, hk), lambda i, k: (i, 0)),
            pl.BlockSpec((tm, hk), lambda i, k: (i, 0)),
            pl.BlockSpec((tm, hk), lambda i, k: (i, 0)),
            pl.BlockSpec((d_model, ec), lambda i, k: (0, k)),
            pl.BlockSpec((ec, d_model), lambda i, k: (k, 0)),
            pl.BlockSpec((1, d_model), lambda i, k: (0, 0)),
            pl.BlockSpec((1, d_model), lambda i, k: (0, 0)),
        ],
        out_specs=pl.BlockSpec((tm, d_model), lambda i, k: (i, 0)),
        out_shape=jax.ShapeDtypeStruct((t, d_model), F32),
        scratch_shapes=[
            pltpu.VMEM((n_keys, tm, n_keys), BF16),
            pltpu.VMEM((tm, d_model), F32),
            pltpu.VMEM((tm, d_model), BF16),
        ],
        compiler_params=pltpu.CompilerParams(
            dimension_semantics=("arbitrary", "arbitrary"),
            vmem_limit_bytes=V7X_VMEM_LIMIT_BYTES),
        name="peer_experts",
    )(x1, gates, cidx, jidx, ut_bf16, v_bf16, ln_g, ln_b)


def _largest_tile(n, cap, mult):
    best = None
    for c in range(mult, min(n, cap) + 1, mult):
        if n % c == 0:
            best = c
    assert best is not None, (n, cap, mult)
    return best


def _block_diag(w):
    h, a, b = w.shape
    return jnp.einsum("hij,hg->higj", w, jnp.eye(h, dtype=w.dtype)).reshape(h * a, h * b)


def kernel(x_prompt, x_sample, mem_prompt, cache_mem_k, cache_mem_v, state_lru_conv, state_lru_h, state_sc_conv, w_in, lru_conv_w, lru_conv_b, w_rg_a, b_rg_a, w_rg_x, b_rg_x, lru_lambda, sc_conv_w, w_mem_kv, g_lru, g_sc, g_mem, w_out, ln1_g, ln1_b, peer_wq, peer_keys, peer_u, peer_v, ln2_g, ln2_b):
    depth = w_in.shape[0]
    alpha = (2.0 * depth) ** 0.25
    bsz, seq, d_model = x_prompt.shape
    dbsz, dseq, _ = x_sample.shape
    n_mem, mem_heads, mem_hd = cache_mem_k.shape[2:]
    d_mem = mem_heads * mem_hd
    d_lru = state_lru_h.shape[-1]
    d_sc = state_sc_conv.shape[-1]
    n_heads, _, n_keys, dk2 = peer_keys.shape[1:]
    assert n_keys == V7X_LANES and dk2 == V7X_LANES

    xp, xs = x_prompt, x_sample
    outs = {k: [] for k in ("mk", "mv", "lc_p", "lh_p", "sc_p", "lc_s", "lh_s", "sc_s")}
    row = lambda v: v.reshape(1, -1)
    for l in range(depth):
        wts = dict(
            w_in=w_in[l].astype(BF16), lru_conv_w=lru_conv_w[l], lru_conv_b=row(lru_conv_b[l]),
            wa=_block_diag(w_rg_a[l]).astype(BF16), ba=row(b_rg_a[l]),
            wx=_block_diag(w_rg_x[l]).astype(BF16), bx=row(b_rg_x[l]), lam=row(lru_lambda[l]),
            sc_conv_w=sc_conv_w[l], g_lru=row(g_lru[l]), g_sc=row(g_sc[l]), g_mem=row(g_mem[l]),
            w_out=w_out[l].astype(BF16), ln_g=row(ln1_g[l]), ln_b=row(ln1_b[l]))
        mk, mv = _memkv(mem_prompt, w_mem_kv[l].astype(BF16))
        mix = functools.partial(_mixer, wts=wts, alpha=alpha, n_heads_mem=mem_heads)
        x1p, lc_p, lh_p, sc_p = mix(
            xp, mk, mv,
            jnp.zeros((bsz,) + state_lru_conv.shape[2:], F32), jnp.zeros((bsz, d_lru), F32),
            jnp.zeros((bsz,) + state_sc_conv.shape[2:], F32),
            nb=bsz, ts=_largest_tile(seq, 128, V7X_SUBLANES))
        x1s, lc_s, lh_s, sc_s = mix(
            xs, cache_mem_k[l].reshape(dbsz, n_mem, d_mem), cache_mem_v[l].reshape(dbsz, n_mem, d_mem),
            state_lru_conv[l], state_lru_h[l], state_sc_conv[l],
            nb=_largest_tile(dbsz, 16, V7X_SUBLANES), ts=dseq)

        wq_perm = peer_wq[l].astype(BF16).reshape(d_model, n_heads, 2, dk2).transpose(0, 2, 1, 3)
        wq_perm = wq_perm.reshape(d_model, 2 * n_heads * dk2)
        keys_t = peer_keys[l].astype(BF16).transpose(1, 0, 2, 3)
        ut_bf16 = peer_u[l].T.astype(BF16)
        v_bf16 = peer_v[l].astype(BF16)

        def peer(x1_3d):
            x1 = x1_3d.reshape(-1, d_model)
            t = x1.shape[0]
            gates, cidx, jidx = _peer_topk(x1, wq_perm, keys_t, n_heads=n_heads,
                                           tm=_largest_tile(t, 512, V7X_LANES))
            x2 = _peer_experts(x1, gates, cidx, jidx, ut_bf16, v_bf16, row(ln2_g[l]), row(ln2_b[l]),
                               alpha=alpha, n_keys=n_keys, tm=_largest_tile(t, 512, _GATE_GROUP),
                               ec=16 * n_keys)
            return x2.reshape(x1_3d.shape)

        xp = peer(x1p)
        xs = peer(x1s)

        outs["mk"].append(mk.reshape(bsz, n_mem, mem_heads, mem_hd))
        outs["mv"].append(mv.reshape(bsz, n_mem, mem_heads, mem_hd))
        for name, val in (("lc_p", lc_p), ("lh_p", lh_p), ("sc_p", sc_p),
                          ("lc_s", lc_s), ("lh_s", lh_s), ("sc_s", sc_s)):
            outs[name].append(val)
    st = {k: jnp.stack(v) for k, v in outs.items()}
    return (xp, xs, st["mk"], st["mv"], st["lc_p"], st["lh_p"], st["sc_p"],
            st["lc_s"], st["lh_s"], st["sc_s"])
```

```python
import functools
import math

import jax
import jax.numpy as jnp
from jax import lax
from jax.experimental import pallas as pl
from jax.experimental.pallas import tpu as pltpu

F32 = jnp.float32
BF16 = jnp.bfloat16

LRU_C = 8.0
LN_EPS = 1e-5
RMS_EPS = 1e-6
PEER_TOPK = 16

V7X_LANES = 128
V7X_SUBLANES = 8
V7X_VMEM_LIMIT_BYTES = 56 * 1024 * 1024

_NT = (((1,), (1,)), ((), ()))


def _gelu_exact(x):
    return 0.5 * x * (1.0 + lax.erf(x * (1.0 / math.sqrt(2.0))))


def _layernorm(x, g, b):
    mu = jnp.mean(x, axis=-1, keepdims=True)
    xc = x - mu
    var = jnp.mean(xc * xc, axis=-1, keepdims=True)
    return xc * lax.rsqrt(var + LN_EPS) * g + b


def _rmsnorm(x, g):
    return x * lax.rsqrt(jnp.mean(x * x, axis=-1, keepdims=True) + RMS_EPS) * g


def _memkv_kernel(mem_ref, w_ref, k_ref, v_ref):
    d_mem = k_ref.shape[-1]
    kv = jnp.dot(mem_ref[0].astype(BF16), w_ref[...], preferred_element_type=F32)
    k_ref[0] = kv[:, :d_mem]
    v_ref[0] = kv[:, d_mem:]


def _memkv(mem, w_kv_bf16):
    bsz, n_mem, d_model = mem.shape
    d_mem = w_kv_bf16.shape[1] // 2
    out = jax.ShapeDtypeStruct((bsz, n_mem, d_mem), F32)
    return pl.pallas_call(
        _memkv_kernel,
        grid=(bsz,),
        in_specs=[
            pl.BlockSpec((1, n_mem, d_model), lambda b: (b, 0, 0)),
            pl.BlockSpec((d_model, 2 * d_mem), lambda b: (0, 0)),
        ],
        out_specs=[
            pl.BlockSpec((1, n_mem, d_mem), lambda b: (b, 0, 0)),
            pl.BlockSpec((1, n_mem, d_mem), lambda b: (b, 0, 0)),
        ],
        out_shape=[out, out],
        name="memkv",
    )(mem, w_kv_bf16)


_CONV_PAD = V7X_SUBLANES


def _causal_conv(buf_ref, x3, w_ref, width):
    nb, ts, c = x3.shape
    buf_ref[:, _CONV_PAD:_CONV_PAD + ts, :] = x3
    acc = x3 * w_ref[width - 1:width, :].reshape(1, 1, c)
    for k in range(width - 1):
        start = _CONV_PAD - (width - 1) + k
        acc = acc + buf_ref[:, start:start + ts, :] * w_ref[k:k + 1, :].reshape(1, 1, c)
    tail = buf_ref[:, _CONV_PAD + ts - (width - 1):_CONV_PAD + ts, :]
    buf_ref[:, _CONV_PAD - (width - 1):_CONV_PAD, :] = tail
    return acc, tail


def _mixer_kernel(alpha, n_heads_mem,
                  x_ref, mk_ref, mv_ref, lc0_ref, h0_ref, sc0_ref,
                  w_in_ref, cw_ref, cb_ref, wa_ref, ba_ref, wx_ref, bx_ref, lam_ref,
                  scw_ref, g_lru_ref, g_sc_ref, g_mem_ref, w_out_ref, ln_g_ref, ln_b_ref,
                  y_ref, lc_ref, h_ref, sc_ref,
                  cbuf_a, cbuf_b, a_scr, b_scr, hall, hcar, q_scr, ymem):
    nb, ts, d_model = x_ref.shape
    rows = nb * ts
    d_lru = h_ref.shape[-1]
    d_sc = sc_ref.shape[-1]
    d_mem = mk_ref.shape[-1]
    lru_w = cw_ref.shape[0]
    sc_w = scw_ref.shape[0]
    j = pl.program_id(1)

    @pl.when(j == 0)
    def _():
        cbuf_a[:, _CONV_PAD - (lru_w - 1):_CONV_PAD, :] = lc0_ref[...]
        cbuf_b[:, _CONV_PAD - (sc_w - 1):_CONV_PAD, :] = sc0_ref[...]
        hcar[...] = h0_ref[...]

    x2 = x_ref[...].reshape(rows, d_model)
    proj = jnp.dot(x2.astype(BF16), w_in_ref[...], preferred_element_type=F32)
    o = 0
    xl = proj[:, o:o + d_lru]; o += d_lru
    gate = proj[:, o:o + d_lru]; o += d_lru
    sc_b = proj[:, o:o + d_sc]; o += d_sc
    sc_c = proj[:, o:o + d_sc]; o += d_sc
    sc_x = proj[:, o:o + d_sc]; o += d_sc
    q_scr[...] = proj[:, o:o + d_mem]

    xc3, lc_tail = _causal_conv(cbuf_a, xl.reshape(nb, ts, d_lru), cw_ref, lru_w)
    lc_ref[...] = lc_tail
    xc = xc3.reshape(rows, d_lru) + cb_ref[...]
    xcb = xc.astype(BF16)
    r = jax.nn.sigmoid(jnp.dot(xcb, wa_ref[...], preferred_element_type=F32) + ba_ref[...])
    i = jax.nn.sigmoid(jnp.dot(xcb, wx_ref[...], preferred_element_type=F32) + bx_ref[...])
    lam = lam_ref[...]
    log_sig = -(jnp.maximum(-lam, 0.0) + jnp.log1p(jnp.exp(-jnp.abs(lam))))
    log_a = LRU_C * r * log_sig
    a = jnp.exp(log_a)
    bt = jnp.sqrt(-jnp.tanh(log_a) * (a * a + 1.0)) * (i * xc)
    n_lt = d_lru // V7X_LANES
    a3 = a.reshape(nb, ts, d_lru)
    b3 = bt.reshape(nb, ts, d_lru)
    for c in range(n_lt):
        lt = slice(c * V7X_LANES, (c + 1) * V7X_LANES)
        a_scr[c] = jnp.transpose(a3[:, :, lt], (1, 0, 2))
        b_scr[c] = jnp.transpose(b3[:, :, lt], (1, 0, 2))

    def scan_step(s, hs):
        new = []
        for c in range(n_lt):
            h = a_scr[c, s] * hs[c] + b_scr[c, s]
            hall[c, s] = h
            new.append(h)
        return tuple(new)

    h0 = hcar[...]
    hs = lax.fori_loop(0, ts, scan_step,
                       tuple(h0[:, c * V7X_LANES:(c + 1) * V7X_LANES] for c in range(n_lt)),
                       unroll=8)
    h_last = jnp.concatenate(hs, axis=-1)
    hcar[...] = h_last
    h_ref[...] = h_last
    h_all = jnp.concatenate(
        [jnp.transpose(hall[c], (1, 0, 2)).reshape(rows, V7X_LANES) for c in range(n_lt)], axis=-1)
    y_lru = h_all * _gelu_exact(gate)

    u3, sc_tail = _causal_conv(cbuf_b, (sc_c * sc_x).reshape(nb, ts, d_sc), scw_ref, sc_w)
    sc_ref[...] = sc_tail
    y_sc = sc_b * u3.reshape(rows, d_sc)

    hd = d_mem // n_heads_mem
    head_of_lane = lax.broadcasted_iota(jnp.int32, (ts, d_mem), 1) >> int(math.log2(hd))
    scale = hd ** -0.5

    def attn_step(b, carry):
        rws = pl.ds(pl.multiple_of(b * ts, ts), ts)
        qb = q_scr[rws, :]
        qm = jnp.concatenate(
            [jnp.where(head_of_lane == h, qb, 0.0) for h in range(n_heads_mem)], axis=0)
        kb = mk_ref[b].astype(BF16)
        vb = mv_ref[b].astype(BF16)
        s = lax.dot_general(qm.astype(BF16), kb, _NT, preferred_element_type=F32) * scale
        s = s - jnp.max(s, axis=-1, keepdims=True)
        p = jnp.exp(s)
        p = p / jnp.sum(p, axis=-1, keepdims=True)
        yh = jnp.dot(p.astype(BF16), vb, preferred_element_type=F32)
        yb = jnp.zeros((ts, d_mem), F32)
        for h in range(n_heads_mem):
            yb = yb + jnp.where(head_of_lane == h, yh[h * ts:(h + 1) * ts, :], 0.0)
        ymem[rws, :] = yb
        return carry

    lax.fori_loop(0, nb, attn_step, 0, unroll=2)

    y = jnp.concatenate(
        [_rmsnorm(y_lru, g_lru_ref[...]), _rmsnorm(y_sc, g_sc_ref[...]),
         _rmsnorm(ymem[...], g_mem_ref[...])], axis=-1)
    out = jnp.dot(y.astype(BF16), w_out_ref[...], preferred_element_type=F32)
    y_ref[...] = _layernorm(alpha * x2 + out, ln_g_ref[...], ln_b_ref[...]).reshape(nb, ts, d_model)


def _const_spec(shape):
    nd = len(shape)
    return pl.BlockSpec(shape, lambda i, j: (0,) * nd)


def _mixer(x, mem_k, mem_v, lc0, h0, sc0, wts, *, alpha, n_heads_mem, nb, ts):
    bsz, seq, d_model = x.shape
    n_mem, d_mem = mem_k.shape[1:]
    d_lru = h0.shape[-1]
    d_sc = sc0.shape[-1]
    lru_w = lc0.shape[1] + 1
    sc_w = sc0.shape[1] + 1
    rows = nb * ts
    grid = (bsz // nb, seq // ts)
    weights = [wts[k] for k in ("w_in", "lru_conv_w", "lru_conv_b", "wa", "ba", "wx", "bx", "lam",
                                "sc_conv_w", "g_lru", "g_sc", "g_mem", "w_out", "ln_g", "ln_b")]
    in_specs = [
        pl.BlockSpec((nb, ts, d_model), lambda i, j: (i, j, 0)),
        pl.BlockSpec((nb, n_mem, d_mem), lambda i, j: (i, 0, 0)),
        pl.BlockSpec((nb, n_mem, d_mem), lambda i, j: (i, 0, 0)),
        pl.BlockSpec((nb, lru_w - 1, d_lru), lambda i, j: (i, 0, 0)),
        pl.BlockSpec((nb, d_lru), lambda i, j: (i, 0)),
        pl.BlockSpec((nb, sc_w - 1, d_sc), lambda i, j: (i, 0, 0)),
    ] + [_const_spec(w.shape) for w in weights]
    out_specs = [
        pl.BlockSpec((nb, ts, d_model), lambda i, j: (i, j, 0)),
        pl.BlockSpec((nb, lru_w - 1, d_lru), lambda i, j: (i, 0, 0)),
        pl.BlockSpec((nb, d_lru), lambda i, j: (i, 0)),
        pl.BlockSpec((nb, sc_w - 1, d_sc), lambda i, j: (i, 0, 0)),
    ]
    out_shape = [
        jax.ShapeDtypeStruct((bsz, seq, d_model), F32),
        jax.ShapeDtypeStruct((bsz, lru_w - 1, d_lru), F32),
        jax.ShapeDtypeStruct((bsz, d_lru), F32),
        jax.ShapeDtypeStruct((bsz, sc_w - 1, d_sc), F32),
    ]
    scratch = [
        pltpu.VMEM((nb, ts + _CONV_PAD, d_lru), F32),
        pltpu.VMEM((nb, ts + _CONV_PAD, d_sc), F32),
        pltpu.VMEM((d_lru // V7X_LANES, ts, nb, V7X_LANES), F32),
        pltpu.VMEM((d_lru // V7X_LANES, ts, nb, V7X_LANES), F32),
        pltpu.VMEM((d_lru // V7X_LANES, ts, nb, V7X_LANES), F32),
        pltpu.VMEM((nb, d_lru), F32),
        pltpu.VMEM((rows, d_mem), F32),
        pltpu.VMEM((rows, d_mem), F32),
    ]
    return pl.pallas_call(
        functools.partial(_mixer_kernel, alpha, n_heads_mem),
        grid=grid,
        in_specs=in_specs,
        out_specs=out_specs,
        out_shape=out_shape,
        scratch_shapes=scratch,
        compiler_params=pltpu.CompilerParams(
            dimension_semantics=("arbitrary", "arbitrary"),
            vmem_limit_bytes=V7X_VMEM_LIMIT_BYTES),
        name="mixer",
    )(x, mem_k, mem_v, lc0, h0, sc0, *weights)


def _argmax_tree(nodes):
    nodes = list(nodes)
    while len(nodes) > 1:
        nxt = []
        for i in range(0, len(nodes) - 1, 2):
            (av, ak), (bv, bk) = nodes[i], nodes[i + 1]
            nxt.append((jnp.maximum(av, bv), jnp.where(av >= bv, ak, bk)))
        if len(nodes) % 2:
            nxt.append(nodes[-1])
        nodes = nxt
    return nodes[0]


def _pair_list():
    k = PEER_TOPK
    return [(a, b) for a in range(k) for b in range(k // (a + 1))]


def _sort_network(n):
    pairs = []

    def merge(lo, hi, r):
        step = r * 2
        if step < hi - lo:
            merge(lo, hi, step)
            merge(lo + r, hi, step)
            pairs.extend((i, i + r) for i in range(lo + r, hi - r, step))
        else:
            pairs.append((lo, lo + r))

    def sort(lo, hi):
        if hi - lo >= 1:
            mid = lo + (hi - lo) // 2
            sort(lo, mid)
            sort(mid + 1, hi)
            merge(lo, hi, 1)

    sort(0, n - 1)
    return tuple(pairs)


def _compare_exchange(v, k, i, j):
    ge = v[i] >= v[j]
    v[i], v[j] = jnp.maximum(v[i], v[j]), jnp.minimum(v[i], v[j])
    k[i], k[j] = jnp.where(ge, k[i], k[j]), jnp.where(ge, k[j], k[i])


def _sorted_top16(rows):
    k = PEER_TOPK
    net = _sort_network(k)
    pad = -len(rows) % k
    rows = list(rows) + [(jnp.full_like(rows[0][0], -jnp.inf), float("inf"))] * pad
    groups = []
    for g0 in range(0, len(rows), k):
        v = [r[0] for r in rows[g0:g0 + k]]
        kk = [r[1] for r in rows[g0:g0 + k]]
        for i, j in net:
            _compare_exchange(v, kk, i, j)
        groups.append((v, kk))
    tv, tk = groups[0]
    dropped = None
    for bv, bk in groups[1:]:
        cv, ck = [], []
        for i in range(k):
            a, b = tv[i], bv[k - 1 - i]
            cv.append(jnp.maximum(a, b))
            ck.append(jnp.where(a >= b, tk[i], bk[k - 1 - i]))
            lo = jnp.minimum(a, b)
            dropped = lo if dropped is None else jnp.maximum(dropped, lo)
        d = k // 2
        while d >= 1:
            for i in range(k):
                if not i & d:
                    _compare_exchange(cv, ck, i, i + d)
            d //= 2
        tv, tk = cv, ck
    for start in (0, 1):
        for i in range(start, k - 1, 2):
            swap = (tv[i] == tv[i + 1]) & (tk[i] > tk[i + 1])
            tk[i], tk[i + 1] = jnp.where(swap, tk[i + 1], tk[i]), jnp.where(swap, tk[i], tk[i + 1])
    suspect = jnp.where(tv[k - 1] <= dropped, 1.0, 0.0)
    for i in range(k - 2):
        suspect = jnp.maximum(suspect, jnp.where(tv[i] == tv[i + 2], 1.0, 0.0))
    return tv, tk, suspect


def _topk_kernel(n_keys, n_heads, x_ref, wq_ref, kb_ref, g_ref, c_ref, j_ref,
                 q_scr, sa_scr, sb_scr, v_scr, i_scr, cv_scr, ck_scr, ts_scr, tk_scr):
    tm = x_ref.shape[0]
    lanes = V7X_LANES
    k = PEER_TOPK
    n_experts = n_keys * n_keys
    dk2 = kb_ref.shape[3]
    n_blk = tm // lanes
    pairs = _pair_list()
    q_scr[...] = jnp.dot(x_ref[...].astype(BF16), wq_ref[...],
                         preferred_element_type=F32).astype(BF16)
    none = jnp.full((n_heads, lanes), -1.0, F32)

    def write_scores(blk, s_scr):
        q = q_scr[pl.ds(pl.multiple_of(blk * lanes, lanes), lanes), :]
        for p in range(2):
            per_head = []
            for h in range(n_heads):
                col = (p * n_heads + h) * dk2
                per_head.append(lax.dot_general(kb_ref[p, h], q[:, col:col + dk2], _NT,
                                                preferred_element_type=F32))
            s_scr[p] = jnp.transpose(jnp.stack(per_head), (1, 0, 2))

    def select(blk, s_scr, next_blk, next_scr):
        rows = pl.ds(pl.multiple_of(blk * lanes, lanes), lanes)
        write_scores(next_blk, next_scr)

        suspect = None
        for p in range(2):
            tv, tk, sus = _sorted_top16([(s_scr[p, key], float(key)) for key in range(n_keys)])
            for r in range(k):
                v_scr[p, r] = tv[r]
                i_scr[p, r] = tk[r]
            suspect = sus if suspect is None else jnp.maximum(suspect, sus)

        @pl.when(jnp.max(suspect) > 0.0)
        def _():
            def rank_body(it, prev):
                new = []
                for p in range(2):
                    leaves = []
                    for key in range(n_keys):
                        v = jnp.where(prev[p] == float(key), -jnp.inf, s_scr[p, key])
                        s_scr[p, key] = v
                        leaves.append((v, float(key)))
                    m, idx = _argmax_tree(leaves)
                    v_scr[p, it] = m
                    i_scr[p, it] = idx
                    new.append(idx)
                return tuple(new)

            lax.fori_loop(0, k, rank_body, (none, none))

        for n, (a, b) in enumerate(pairs):
            cv_scr[n] = v_scr[0, a] + v_scr[1, b]
            ck_scr[n] = float((a * k + b) * n_experts) + (i_scr[0, a] * float(n_keys) + i_scr[1, b])
        tv, tk, sus = _sorted_top16([(cv_scr[n], ck_scr[n]) for n in range(len(pairs))])
        for r in range(k):
            ts_scr[r] = tv[r]
            tk_scr[r] = tk[r]

        @pl.when(jnp.max(sus) > 0.0)
        def _():
            def pair_body(it, prev):
                leaves = []
                for n in range(len(pairs)):
                    kc = ck_scr[n]
                    v = jnp.where(kc == prev, -jnp.inf, cv_scr[n])
                    cv_scr[n] = v
                    leaves.append((v, kc))
                m, kk = _argmax_tree(leaves)
                ts_scr[it] = m
                tk_scr[it] = kk
                return kk

            lax.fori_loop(0, k, pair_body, none)

        top = ts_scr[...]
        ex = jnp.exp(top - top[0:1])
        gates = ex / jnp.sum(ex, axis=0, keepdims=True)
        e = tk_scr[...].astype(jnp.int32) & (n_experts - 1)
        g_ref[rows, :] = gates.reshape(k * n_heads, lanes).T
        c_ref[rows, :] = (e >> int(math.log2(n_keys))).astype(F32).reshape(k * n_heads, lanes).T
        j_ref[rows, :] = (e & (n_keys - 1)).astype(F32).reshape(k * n_heads, lanes).T

    write_scores(0, sa_scr)

    def pair_of_blocks(i, carry):
        blk = 2 * i
        select(blk, sa_scr, blk + 1, sb_scr)
        select(blk + 1, sb_scr, jnp.minimum(blk + 2, n_blk - 1), sa_scr)
        return carry

    lax.fori_loop(0, n_blk // 2, pair_of_blocks, 0)


def _peer_topk(x1, wq_perm, keys_t, *, n_heads, tm):
    t, d_model = x1.shape
    n_keys = keys_t.shape[2]
    hk = n_heads * PEER_TOPK
    n_pairs = len(_pair_list())
    out = jax.ShapeDtypeStruct((t, hk), F32)
    vreg = (n_heads, V7X_LANES)
    assert tm % (2 * V7X_LANES) == 0
    return pl.pallas_call(
        functools.partial(_topk_kernel, n_keys, n_heads),
        grid=(t // tm,),
        in_specs=[
            pl.BlockSpec((tm, d_model), lambda i: (i, 0)),
            pl.BlockSpec(wq_perm.shape, lambda i: (0, 0)),
            pl.BlockSpec(keys_t.shape, lambda i: (0, 0, 0, 0)),
        ],
        out_specs=[pl.BlockSpec((tm, hk), lambda i: (i, 0))] * 3,
        out_shape=[out, out, out],
        scratch_shapes=[
            pltpu.VMEM((tm, wq_perm.shape[1]), BF16),
            pltpu.VMEM((2, n_keys) + vreg, F32),
            pltpu.VMEM((2, n_keys) + vreg, F32),
            pltpu.VMEM((2, PEER_TOPK) + vreg, F32),
            pltpu.VMEM((2, PEER_TOPK) + vreg, F32),
            pltpu.VMEM((n_pairs,) + vreg, F32),
            pltpu.VMEM((n_pairs,) + vreg, F32),
            pltpu.VMEM((PEER_TOPK,) + vreg, F32),
            pltpu.VMEM((PEER_TOPK,) + vreg, F32),
        ],
        compiler_params=pltpu.CompilerParams(
            dimension_semantics=("arbitrary",),
            vmem_limit_bytes=V7X_VMEM_LIMIT_BYTES),
        name="peer_topk",
    )(x1, wq_perm, keys_t)


_GATE_GROUP = 2 * V7X_SUBLANES


def _experts_kernel(alpha, n_keys, x_ref, g_ref, c_ref, j_ref, ut_ref, v_ref, ln_g_ref, ln_b_ref,
                    o_ref, gate_scr, acc, xb_scr):
    tm, d_model = x_ref.shape
    ec = ut_ref.shape[1]
    chunks_c = ec // n_keys
    kstep = pl.program_id(1)

    @pl.when(kstep == 0)
    def _():
        xb_scr[...] = x_ref[...].astype(BF16)
        acc[...] = jnp.zeros_like(acc)
        row_iota = lax.broadcasted_iota(jnp.int32, (n_keys, g_ref.shape[1]), 0).astype(F32)

        def group_body(tg, carry):
            base = pl.multiple_of(tg * _GATE_GROUP, _GATE_GROUP)
            per_token = []
            for tl in range(_GATE_GROUP):
                row = pl.ds(base + tl, 1)
                ct = jnp.where(c_ref[row, :] == row_iota, g_ref[row, :], 0.0).astype(BF16)
                jt = jnp.where(j_ref[row, :] == row_iota, 1.0, 0.0).astype(BF16)
                per_token.append(lax.dot_general(ct, jt, _NT, preferred_element_type=F32).astype(BF16))
            gate_scr[:, pl.ds(base, _GATE_GROUP), :] = jnp.transpose(jnp.stack(per_token), (1, 0, 2))
            return carry

        lax.fori_loop(0, tm // _GATE_GROUP, group_body, 0, unroll=4)

    a = jnp.dot(xb_scr[...], ut_ref[...], preferred_element_type=F32)
    gate = jnp.concatenate([gate_scr[kstep * chunks_c + i] for i in range(chunks_c)], axis=-1)
    w = _gelu_exact(a).astype(BF16) * gate
    acc[...] += jnp.dot(w, v_ref[...], preferred_element_type=F32)

    @pl.when(kstep == pl.num_programs(1) - 1)
    def _():
        o_ref[...] = _layernorm(alpha * x_ref[...] + acc[...], ln_g_ref[...], ln_b_ref[...])


def _peer_experts(x1, gates, cidx, jidx, ut_bf16, v_bf16, ln_g, ln_b, *, alpha, n_keys, tm, ec):
    t, d_model = x1.shape
    n_exp = v_bf16.shape[0]
    hk = gates.shape[1]
    n_chunks = n_exp // ec
    return pl.pallas_call(
        functools.partial(_experts_kernel, alpha, n_keys),
        grid=(t // tm, n_chunks),
        in_specs=[
            pl.BlockSpec((tm, d_model), lambda i, k: (i, 0)),
            pl.BlockSpec((tm, hk), lambda i, k: (i, 0)),
            pl.BlockSpec((tm, hk), lambda i, k: (i, 0)),
            pl.BlockSpec((tm, hk), lambda i, k: (i, 0)),
            pl.BlockSpec((d_model, ec), lambda i, k: (0, k)),
            pl.BlockSpec((ec, d_model), lambda i, k: (k, 0)),
            pl.BlockSpec((1, d_model), lambda i, k: (0, 0)),
            pl.BlockSpec((1, d_model), lambda i, k: (0, 0)),
        ],
        out_specs=pl.BlockSpec((tm, d_model), lambda i, k: (i, 0)),
        out_shape=jax.ShapeDtypeStruct((t, d_model), F32),
        scratch_shapes=[
            pltpu.VMEM((n_keys, tm, n_keys), BF16),
            pltpu.VMEM((tm, d_model), F32),
            pltpu.VMEM((tm, d_model), BF16),
        ],
        compiler_params=pltpu.CompilerParams(
            dimension_semantics=("arbitrary", "arbitrary"),
            vmem_limit_bytes=V7X_VMEM_LIMIT_BYTES),
        name="peer_experts",
    )(x1, gates, cidx, jidx, ut_bf16, v_bf16, ln_g, ln_b)


def _largest_tile(n, cap, mult):
    best = None
    for c in range(mult, min(n, cap) + 1, mult):
        if n % c == 0:
            best = c
    assert best is not None, (n, cap, mult)
    return best


def _block_diag(w):
    h, a, b = w.shape
    return jnp.einsum("hij,hg->higj", w, jnp.eye(h, dtype=w.dtype)).reshape(h * a, h * b)


def kernel(x_prompt, x_sample, mem_prompt, cache_mem_k, cache_mem_v, state_lru_conv, state_lru_h, state_sc_conv, w_in, lru_conv_w, lru_conv_b, w_rg_a, b_rg_a, w_rg_x, b_rg_x, lru_lambda, sc_conv_w, w_mem_kv, g_lru, g_sc, g_mem, w_out, ln1_g, ln1_b, peer_wq, peer_keys, peer_u, peer_v, ln2_g, ln2_b):
    depth = w_in.shape[0]
    alpha = (2.0 * depth) ** 0.25
    bsz, seq, d_model = x_prompt.shape
    dbsz, dseq, _ = x_sample.shape
    n_mem, mem_heads, mem_hd = cache_mem_k.shape[2:]
    d_mem = mem_heads * mem_hd
    d_lru = state_lru_h.shape[-1]
    d_sc = state_sc_conv.shape[-1]
    n_heads, _, n_keys, dk2 = peer_keys.shape[1:]
    assert n_keys == V7X_LANES and dk2 == V7X_LANES

    xp, xs = x_prompt, x_sample
    outs = {k: [] for k in ("mk", "mv", "lc_p", "lh_p", "sc_p", "lc_s", "lh_s", "sc_s")}
    row = lambda v: v.reshape(1, -1)
    for l in range(depth):
        wts = dict(
            w_in=w_in[l].astype(BF16), lru_conv_w=lru_conv_w[l], lru_conv_b=row(lru_conv_b[l]),
            wa=_block_diag(w_rg_a[l]).astype(BF16), ba=row(b_rg_a[l]),
            wx=_block_diag(w_rg_x[l]).astype(BF16), bx=row(b_rg_x[l]), lam=row(lru_lambda[l]),
            sc_conv_w=sc_conv_w[l], g_lru=row(g_lru[l]), g_sc=row(g_sc[l]), g_mem=row(g_mem[l]),
            w_out=w_out[l].astype(BF16), ln_g=row(ln1_g[l]), ln_b=row(ln1_b[l]))
        mk, mv = _memkv(mem_prompt, w_mem_kv[l].astype(BF16))
        mix = functools.partial(_mixer, wts=wts, alpha=alpha, n_heads_mem=mem_heads)
        x1p, lc_p, lh_p, sc_p = mix(
            xp, mk, mv,
            jnp.zeros((bsz,) + state_lru_conv.shape[2:], F32), jnp.zeros((bsz, d_lru), F32),
            jnp.zeros((bsz,) + state_sc_conv.shape[2:], F32),
            nb=bsz, ts=_largest_tile(seq, 128, V7X_SUBLANES))
        x1s, lc_s, lh_s, sc_s = mix(
            xs, cache_mem_k[l].reshape(dbsz, n_mem, d_mem), cache_mem_v[l].reshape(dbsz, n_mem, d_mem),
            state_lru_conv[l], state_lru_h[l], state_sc_conv[l],
            nb=_largest_tile(dbsz, 16, V7X_SUBLANES), ts=dseq)

        wq_perm = peer_wq[l].astype(BF16).reshape(d_model, n_heads, 2, dk2).transpose(0, 2, 1, 3)
        wq_perm = wq_perm.reshape(d_model, 2 * n_heads * dk2)
        keys_t = peer_keys[l].astype(BF16).transpose(1, 0, 2, 3)
        ut_bf16 = peer_u[l].T.astype(BF16)
        v_bf16 = peer_v[l].astype(BF16)

        def peer(x1_3d):
            x1 = x1_3d.reshape(-1, d_model)
            t = x1.shape[0]
            gates, cidx, jidx = _peer_topk(x1, wq_perm, keys_t, n_heads=n_heads,
                                           tm=_largest_tile(t, 512, 2 * V7X_LANES))
            x2 = _peer_experts(x1, gates, cidx, jidx, ut_bf16, v_bf16, row(ln2_g[l]), row(ln2_b[l]),
                               alpha=alpha, n_keys=n_keys, tm=_largest_tile(t, 512, _GATE_GROUP),
                               ec=16 * n_keys)
            return x2.reshape(x1_3d.shape)

        xp = peer(x1p)
        xs = peer(x1s)

        outs["mk"].append(mk.reshape(bsz, n_mem, mem_heads, mem_hd))
        outs["mv"].append(mv.reshape(bsz, n_mem, mem_heads, mem_hd))
        for name, val in (("lc_p", lc_p), ("lh_p", lh_p), ("sc_p", sc_p),
                          ("lc_s", lc_s), ("lh_s", lh_s), ("sc_s", sc_s)):
            outs[name].append(val)
    st = {k: jnp.stack(v) for k, v in outs.items()}
    return (xp, xs, st["mk"], st["mv"], st["lc_p"], st["lh_p"], st["sc_p"],
            st["lc_s"], st["lh_s"], st["sc_s"])
```

```python
import functools
import math

import jax
import jax.numpy as jnp
from jax import lax
from jax.experimental import pallas as pl
from jax.experimental.pallas import tpu as pltpu

F32 = jnp.float32
BF16 = jnp.bfloat16

LRU_C = 8.0
LN_EPS = 1e-5
RMS_EPS = 1e-6
PEER_TOPK = 16

V7X_LANES = 128
V7X_SUBLANES = 8
V7X_VMEM_LIMIT_BYTES = 56 * 1024 * 1024

_NT = (((1,), (1,)), ((), ()))


def _gelu_exact(x):
    return 0.5 * x * (1.0 + lax.erf(x * (1.0 / math.sqrt(2.0))))


def _layernorm(x, g, b):
    mu = jnp.mean(x, axis=-1, keepdims=True)
    xc = x - mu
    var = jnp.mean(xc * xc, axis=-1, keepdims=True)
    return xc * lax.rsqrt(var + LN_EPS) * g + b


def _rmsnorm(x, g):
    return x * lax.rsqrt(jnp.mean(x * x, axis=-1, keepdims=True) + RMS_EPS) * g


def _memkv_kernel(mem_ref, w_ref, k_ref, v_ref):
    d_mem = k_ref.shape[-1]
    kv = jnp.dot(mem_ref[0].astype(BF16), w_ref[...], preferred_element_type=F32)
    k_ref[0] = kv[:, :d_mem]
    v_ref[0] = kv[:, d_mem:]


def _memkv(mem, w_kv_bf16):
    bsz, n_mem, d_model = mem.shape
    d_mem = w_kv_bf16.shape[1] // 2
    out = jax.ShapeDtypeStruct((bsz, n_mem, d_mem), F32)
    return pl.pallas_call(
        _memkv_kernel,
        grid=(bsz,),
        in_specs=[
            pl.BlockSpec((1, n_mem, d_model), lambda b: (b, 0, 0)),
            pl.BlockSpec((d_model, 2 * d_mem), lambda b: (0, 0)),
        ],
        out_specs=[
            pl.BlockSpec((1, n_mem, d_mem), lambda b: (b, 0, 0)),
            pl.BlockSpec((1, n_mem, d_mem), lambda b: (b, 0, 0)),
        ],
        out_shape=[out, out],
        name="memkv",
    )(mem, w_kv_bf16)


_CONV_PAD = V7X_SUBLANES


def _causal_conv(buf_ref, x3, w_ref, width):
    nb, ts, c = x3.shape
    buf_ref[:, _CONV_PAD:_CONV_PAD + ts, :] = x3
    acc = x3 * w_ref[width - 1:width, :].reshape(1, 1, c)
    for k in range(width - 1):
        start = _CONV_PAD - (width - 1) + k
        acc = acc + buf_ref[:, start:start + ts, :] * w_ref[k:k + 1, :].reshape(1, 1, c)
    tail = buf_ref[:, _CONV_PAD + ts - (width - 1):_CONV_PAD + ts, :]
    buf_ref[:, _CONV_PAD - (width - 1):_CONV_PAD, :] = tail
    return acc, tail


def _mixer_kernel(alpha, n_heads_mem,
                  x_ref, mk_ref, mv_ref, lc0_ref, h0_ref, sc0_ref,
                  w_in_ref, cw_ref, cb_ref, wa_ref, ba_ref, wx_ref, bx_ref, lam_ref,
                  scw_ref, g_lru_ref, g_sc_ref, g_mem_ref, w_out_ref, ln_g_ref, ln_b_ref,
                  y_ref, lc_ref, h_ref, sc_ref,
                  cbuf_a, cbuf_b, a_scr, b_scr, hall, hcar, q_scr, ymem, s_scr=None, p_scr=None):
    nb, ts, d_model = x_ref.shape
    rows = nb * ts
    d_lru = h_ref.shape[-1]
    d_sc = sc_ref.shape[-1]
    d_mem = mk_ref.shape[-1]
    lru_w = cw_ref.shape[0]
    sc_w = scw_ref.shape[0]
    j = pl.program_id(1)

    @pl.when(j == 0)
    def _():
        cbuf_a[:, _CONV_PAD - (lru_w - 1):_CONV_PAD, :] = lc0_ref[...]
        cbuf_b[:, _CONV_PAD - (sc_w - 1):_CONV_PAD, :] = sc0_ref[...]
        hcar[...] = h0_ref[...]

    x2 = x_ref[...].reshape(rows, d_model)
    proj = jnp.dot(x2.astype(BF16), w_in_ref[...], preferred_element_type=F32)
    o = 0
    xl = proj[:, o:o + d_lru]; o += d_lru
    gate = proj[:, o:o + d_lru]; o += d_lru
    sc_b = proj[:, o:o + d_sc]; o += d_sc
    sc_c = proj[:, o:o + d_sc]; o += d_sc
    sc_x = proj[:, o:o + d_sc]; o += d_sc
    q_scr[...] = proj[:, o:o + d_mem]

    xc3, lc_tail = _causal_conv(cbuf_a, xl.reshape(nb, ts, d_lru), cw_ref, lru_w)
    lc_ref[...] = lc_tail
    xc = xc3.reshape(rows, d_lru) + cb_ref[...]
    xcb = xc.astype(BF16)
    r = jax.nn.sigmoid(jnp.dot(xcb, wa_ref[...], preferred_element_type=F32) + ba_ref[...])
    i = jax.nn.sigmoid(jnp.dot(xcb, wx_ref[...], preferred_element_type=F32) + bx_ref[...])
    lam = lam_ref[...]
    log_sig = -(jnp.maximum(-lam, 0.0) + jnp.log1p(jnp.exp(-jnp.abs(lam))))
    log_a = LRU_C * r * log_sig
    a = jnp.exp(log_a)
    bt = jnp.sqrt(-jnp.tanh(log_a) * (a * a + 1.0)) * (i * xc)
    n_lt = d_lru // V7X_LANES
    a3 = a.reshape(nb, ts, d_lru)
    b3 = bt.reshape(nb, ts, d_lru)
    for c in range(n_lt):
        lt = slice(c * V7X_LANES, (c + 1) * V7X_LANES)
        a_scr[c] = jnp.transpose(a3[:, :, lt], (1, 0, 2))
        b_scr[c] = jnp.transpose(b3[:, :, lt], (1, 0, 2))

    def scan_step(s, hs):
        new = []
        for c in range(n_lt):
            h = a_scr[c, s] * hs[c] + b_scr[c, s]
            hall[c, s] = h
            new.append(h)
        return tuple(new)

    h0 = hcar[...]
    hs = lax.fori_loop(0, ts, scan_step,
                       tuple(h0[:, c * V7X_LANES:(c + 1) * V7X_LANES] for c in range(n_lt)),
                       unroll=8)
    h_last = jnp.concatenate(hs, axis=-1)
    hcar[...] = h_last
    h_ref[...] = h_last
    h_all = jnp.concatenate(
        [jnp.transpose(hall[c], (1, 0, 2)).reshape(rows, V7X_LANES) for c in range(n_lt)], axis=-1)
    y_lru = h_all * _gelu_exact(gate)

    u3, sc_tail = _causal_conv(cbuf_b, (sc_c * sc_x).reshape(nb, ts, d_sc), scw_ref, sc_w)
    sc_ref[...] = sc_tail
    y_sc = sc_b * u3.reshape(rows, d_sc)

    hd = d_mem // n_heads_mem
    head_of_lane = lax.broadcasted_iota(jnp.int32, (ts, d_mem), 1) >> int(math.log2(hd))
    scale = hd ** -0.5

    n_rows_b = n_heads_mem * ts

    def seq_scores(b):
        qb = q_scr[pl.ds(pl.multiple_of(b * ts, ts), ts), :]
        qm = jnp.concatenate(
            [jnp.where(head_of_lane == h, qb, 0.0) for h in range(n_heads_mem)], axis=0)
        return lax.dot_general(qm.astype(BF16), mk_ref[b].astype(BF16), _NT,
                               preferred_element_type=F32) * scale

    def softmax_rows(s):
        p = jnp.exp(s - jnp.max(s, axis=-1, keepdims=True))
        return (p / jnp.sum(p, axis=-1, keepdims=True)).astype(BF16)

    def seq_values(b, pb):
        yh = jnp.dot(pb, mv_ref[b].astype(BF16), preferred_element_type=F32)
        yb = jnp.zeros((ts, d_mem), F32)
        for h in range(n_heads_mem):
            yb = yb + jnp.where(head_of_lane == h, yh[h * ts:(h + 1) * ts, :], 0.0)
        ymem[pl.ds(pl.multiple_of(b * ts, ts), ts), :] = yb

    if s_scr is None:
        def attn_step(b, carry):
            seq_values(b, softmax_rows(seq_scores(b)))
            return carry

        lax.fori_loop(0, nb, attn_step, 0, unroll=2)
    else:
        def score_step(b, carry):
            s_scr[pl.ds(pl.multiple_of(b * n_rows_b, n_rows_b), n_rows_b), :] = seq_scores(b)
            return carry

        lax.fori_loop(0, nb, score_step, 0, unroll=2)
        p_scr[...] = softmax_rows(s_scr[...])

        def value_step(b, carry):
            seq_values(b, p_scr[pl.ds(pl.multiple_of(b * n_rows_b, n_rows_b), n_rows_b), :])
            return carry

        lax.fori_loop(0, nb, value_step, 0, unroll=2)

    y = jnp.concatenate(
        [_rmsnorm(y_lru, g_lru_ref[...]), _rmsnorm(y_sc, g_sc_ref[...]),
         _rmsnorm(ymem[...], g_mem_ref[...])], axis=-1)
    out = jnp.dot(y.astype(BF16), w_out_ref[...], preferred_element_type=F32)
    y_ref[...] = _layernorm(alpha * x2 + out, ln_g_ref[...], ln_b_ref[...]).reshape(nb, ts, d_model)


def _const_spec(shape):
    nd = len(shape)
    return pl.BlockSpec(shape, lambda i, j: (0,) * nd)


def _mixer(x, mem_k, mem_v, lc0, h0, sc0, wts, *, alpha, n_heads_mem, nb, ts):
    bsz, seq, d_model = x.shape
    n_mem, d_mem = mem_k.shape[1:]
    d_lru = h0.shape[-1]
    d_sc = sc0.shape[-1]
    lru_w = lc0.shape[1] + 1
    sc_w = sc0.shape[1] + 1
    rows = nb * ts
    grid = (bsz // nb, seq // ts)
    weights = [wts[k] for k in ("w_in", "lru_conv_w", "lru_conv_b", "wa", "ba", "wx", "bx", "lam",
                                "sc_conv_w", "g_lru", "g_sc", "g_mem", "w_out", "ln_g", "ln_b")]
    in_specs = [
        pl.BlockSpec((nb, ts, d_model), lambda i, j: (i, j, 0)),
        pl.BlockSpec((nb, n_mem, d_mem), lambda i, j: (i, 0, 0)),
        pl.BlockSpec((nb, n_mem, d_mem), lambda i, j: (i, 0, 0)),
        pl.BlockSpec((nb, lru_w - 1, d_lru), lambda i, j: (i, 0, 0)),
        pl.BlockSpec((nb, d_lru), lambda i, j: (i, 0)),
        pl.BlockSpec((nb, sc_w - 1, d_sc), lambda i, j: (i, 0, 0)),
    ] + [_const_spec(w.shape) for w in weights]
    out_specs = [
        pl.BlockSpec((nb, ts, d_model), lambda i, j: (i, j, 0)),
        pl.BlockSpec((nb, lru_w - 1, d_lru), lambda i, j: (i, 0, 0)),
        pl.BlockSpec((nb, d_lru), lambda i, j: (i, 0)),
        pl.BlockSpec((nb, sc_w - 1, d_sc), lambda i, j: (i, 0, 0)),
    ]
    out_shape = [
        jax.ShapeDtypeStruct((bsz, seq, d_model), F32),
        jax.ShapeDtypeStruct((bsz, lru_w - 1, d_lru), F32),
        jax.ShapeDtypeStruct((bsz, d_lru), F32),
        jax.ShapeDtypeStruct((bsz, sc_w - 1, d_sc), F32),
    ]
    scratch = [
        pltpu.VMEM((nb, ts + _CONV_PAD, d_lru), F32),
        pltpu.VMEM((nb, ts + _CONV_PAD, d_sc), F32),
        pltpu.VMEM((d_lru // V7X_LANES, ts, nb, V7X_LANES), F32),
        pltpu.VMEM((d_lru // V7X_LANES, ts, nb, V7X_LANES), F32),
        pltpu.VMEM((d_lru // V7X_LANES, ts, nb, V7X_LANES), F32),
        pltpu.VMEM((nb, d_lru), F32),
        pltpu.VMEM((rows, d_mem), F32),
        pltpu.VMEM((rows, d_mem), F32),
    ]
    if ts < V7X_LANES:
        scratch += [pltpu.VMEM((rows * n_heads_mem, n_mem), F32),
                    pltpu.VMEM((rows * n_heads_mem, n_mem), BF16)]
    return pl.pallas_call(
        functools.partial(_mixer_kernel, alpha, n_heads_mem),
        grid=grid,
        in_specs=in_specs,
        out_specs=out_specs,
        out_shape=out_shape,
        scratch_shapes=scratch,
        compiler_params=pltpu.CompilerParams(
            dimension_semantics=("arbitrary", "arbitrary"),
            vmem_limit_bytes=V7X_VMEM_LIMIT_BYTES),
        name="mixer",
    )(x, mem_k, mem_v, lc0, h0, sc0, *weights)


def _argmax_tree(nodes):
    nodes = list(nodes)
    while len(nodes) > 1:
        nxt = []
        for i in range(0, len(nodes) - 1, 2):
            (av, ak), (bv, bk) = nodes[i], nodes[i + 1]
            nxt.append((jnp.maximum(av, bv), jnp.where(av >= bv, ak, bk)))
        if len(nodes) % 2:
            nxt.append(nodes[-1])
        nodes = nxt
    return nodes[0]


def _pair_list():
    k = PEER_TOPK
    return [(a, b) for a in range(k) for b in range(k // (a + 1))]


def _sort_network(n):
    pairs = []

    def merge(lo, hi, r):
        step = r * 2
        if step < hi - lo:
            merge(lo, hi, step)
            merge(lo + r, hi, step)
            pairs.extend((i, i + r) for i in range(lo + r, hi - r, step))
        else:
            pairs.append((lo, lo + r))

    def sort(lo, hi):
        if hi - lo >= 1:
            mid = lo + (hi - lo) // 2
            sort(lo, mid)
            sort(mid + 1, hi)
            merge(lo, hi, 1)

    sort(0, n - 1)
    return tuple(pairs)


def _compare_exchange(v, k, i, j):
    ge = v[i] >= v[j]
    v[i], v[j] = jnp.maximum(v[i], v[j]), jnp.minimum(v[i], v[j])
    k[i], k[j] = jnp.where(ge, k[i], k[j]), jnp.where(ge, k[j], k[i])


def _sorted_top16(rows):
    k = PEER_TOPK
    net = _sort_network(k)
    pad = -len(rows) % k
    rows = list(rows) + [(jnp.full_like(rows[0][0], -jnp.inf), float("inf"))] * pad
    groups = []
    for g0 in range(0, len(rows), k):
        v = [r[0] for r in rows[g0:g0 + k]]
        kk = [r[1] for r in rows[g0:g0 + k]]
        for i, j in net:
            _compare_exchange(v, kk, i, j)
        groups.append((v, kk))
    tv, tk = groups[0]
    dropped = None
    for bv, bk in groups[1:]:
        cv, ck = [], []
        for i in range(k):
            a, b = tv[i], bv[k - 1 - i]
            cv.append(jnp.maximum(a, b))
            ck.append(jnp.where(a >= b, tk[i], bk[k - 1 - i]))
            lo = jnp.minimum(a, b)
            dropped = lo if dropped is None else jnp.maximum(dropped, lo)
        d = k // 2
        while d >= 1:
            for i in range(k):
                if not i & d:
                    _compare_exchange(cv, ck, i, i + d)
            d //= 2
        tv, tk = cv, ck
    for start in (0, 1):
        for i in range(start, k - 1, 2):
            swap = (tv[i] == tv[i + 1]) & (tk[i] > tk[i + 1])
            tk[i], tk[i + 1] = jnp.where(swap, tk[i + 1], tk[i]), jnp.where(swap, tk[i], tk[i + 1])
    suspect = jnp.where(tv[k - 1] <= dropped, 1.0, 0.0)
    for i in range(k - 2):
        suspect = jnp.maximum(suspect, jnp.where(tv[i] == tv[i + 2], 1.0, 0.0))
    return tv, tk, suspect


def _topk_kernel(n_keys, n_heads, x_ref, wq_ref, kb_ref, g_ref, c_ref, j_ref,
                 q_scr, sa_scr, sb_scr, v_scr, i_scr, cv_scr, ck_scr, ts_scr, tk_scr):
    tm = x_ref.shape[0]
    lanes = V7X_LANES
    k = PEER_TOPK
    n_experts = n_keys * n_keys
    dk2 = kb_ref.shape[3]
    n_blk = tm // lanes
    pairs = _pair_list()
    q_scr[...] = jnp.dot(x_ref[...].astype(BF16), wq_ref[...],
                         preferred_element_type=F32).astype(BF16)
    none = jnp.full((n_heads, lanes), -1.0, F32)

    def write_scores(blk, s_scr):
        q = q_scr[pl.ds(pl.multiple_of(blk * lanes, lanes), lanes), :]
        for p in range(2):
            per_head = []
            for h in range(n_heads):
                col = (p * n_heads + h) * dk2
                per_head.append(lax.dot_general(kb_ref[p, h], q[:, col:col + dk2], _NT,
                                                preferred_element_type=F32))
            s_scr[p] = jnp.transpose(jnp.stack(per_head), (1, 0, 2))

    def select(blk, s_scr, next_blk, next_scr):
        rows = pl.ds(pl.multiple_of(blk * lanes, lanes), lanes)
        write_scores(next_blk, next_scr)

        suspect = None
        for p in range(2):
            tv, tk, sus = _sorted_top16([(s_scr[p, key], float(key)) for key in range(n_keys)])
            for r in range(k):
                v_scr[p, r] = tv[r]
                i_scr[p, r] = tk[r]
            suspect = sus if suspect is None else jnp.maximum(suspect, sus)

        @pl.when(jnp.max(suspect) > 0.0)
        def _():
            def rank_body(it, prev):
                new = []
                for p in range(2):
                    leaves = []
                    for key in range(n_keys):
                        v = jnp.where(prev[p] == float(key), -jnp.inf, s_scr[p, key])
                        s_scr[p, key] = v
                        leaves.append((v, float(key)))
                    m, idx = _argmax_tree(leaves)
                    v_scr[p, it] = m
                    i_scr[p, it] = idx
                    new.append(idx)
                return tuple(new)

            lax.fori_loop(0, k, rank_body, (none, none))

        for n, (a, b) in enumerate(pairs):
            cv_scr[n] = v_scr[0, a] + v_scr[1, b]
            ck_scr[n] = float((a * k + b) * n_experts) + (i_scr[0, a] * float(n_keys) + i_scr[1, b])
        tv, tk, sus = _sorted_top16([(cv_scr[n], ck_scr[n]) for n in range(len(pairs))])
        for r in range(k):
            ts_scr[r] = tv[r]
            tk_scr[r] = tk[r]

        @pl.when(jnp.max(sus) > 0.0)
        def _():
            def pair_body(it, prev):
                leaves = []
                for n in range(len(pairs)):
                    kc = ck_scr[n]
                    v = jnp.where(kc == prev, -jnp.inf, cv_scr[n])
                    cv_scr[n] = v
                    leaves.append((v, kc))
                m, kk = _argmax_tree(leaves)
                ts_scr[it] = m
                tk_scr[it] = kk
                return kk

            lax.fori_loop(0, k, pair_body, none)

        top = ts_scr[...]
        ex = jnp.exp(top - top[0:1])
        gates = ex / jnp.sum(ex, axis=0, keepdims=True)
        e = tk_scr[...].astype(jnp.int32) & (n_experts - 1)
        g_ref[rows, :] = gates.reshape(k * n_heads, lanes).T
        c_ref[rows, :] = (e >> int(math.log2(n_keys))).astype(F32).reshape(k * n_heads, lanes).T
        j_ref[rows, :] = (e & (n_keys - 1)).astype(F32).reshape(k * n_heads, lanes).T

    write_scores(0, sa_scr)

    def pair_of_blocks(i, carry):
        blk = 2 * i
        select(blk, sa_scr, blk + 1, sb_scr)
        select(blk + 1, sb_scr, jnp.minimum(blk + 2, n_blk - 1), sa_scr)
        return carry

    lax.fori_loop(0, n_blk // 2, pair_of_blocks, 0)


def _peer_topk(x1, wq_perm, keys_t, *, n_heads, tm):
    t, d_model = x1.shape
    n_keys = keys_t.shape[2]
    hk = n_heads * PEER_TOPK
    n_pairs = len(_pair_list())
    out = jax.ShapeDtypeStruct((t, hk), F32)
    vreg = (n_heads, V7X_LANES)
    assert tm % (2 * V7X_LANES) == 0
    return pl.pallas_call(
        functools.partial(_topk_kernel, n_keys, n_heads),
        grid=(t // tm,),
        in_specs=[
            pl.BlockSpec((tm, d_model), lambda i: (i, 0)),
            pl.BlockSpec(wq_perm.shape, lambda i: (0, 0)),
            pl.BlockSpec(keys_t.shape, lambda i: (0, 0, 0, 0)),
        ],
        out_specs=[pl.BlockSpec((tm, hk), lambda i: (i, 0))] * 3,
        out_shape=[out, out, out],
        scratch_shapes=[
            pltpu.VMEM((tm, wq_perm.shape[1]), BF16),
            pltpu.VMEM((2, n_keys) + vreg, F32),
            pltpu.VMEM((2, n_keys) + vreg, F32),
            pltpu.VMEM((2, PEER_TOPK) + vreg, F32),
            pltpu.VMEM((2, PEER_TOPK) + vreg, F32),
            pltpu.VMEM((n_pairs,) + vreg, F32),
            pltpu.VMEM((n_pairs,) + vreg, F32),
            pltpu.VMEM((PEER_TOPK,) + vreg, F32),
            pltpu.VMEM((PEER_TOPK,) + vreg, F32),
        ],
        compiler_params=pltpu.CompilerParams(
            dimension_semantics=("arbitrary",),
            vmem_limit_bytes=V7X_VMEM_LIMIT_BYTES),
        name="peer_topk",
    )(x1, wq_perm, keys_t)


_GATE_GROUP = 2 * V7X_SUBLANES


def _experts_kernel(alpha, n_keys, x_ref, g_ref, c_ref, j_ref, ut_ref, v_ref, ln_g_ref, ln_b_ref,
                    o_ref, gate_scr, acc, xb_scr):
    tm, d_model = x_ref.shape
    ec = ut_ref.shape[1]
    chunks_c = ec // n_keys
    kstep = pl.program_id(1)

    @pl.when(kstep == 0)
    def _():
        xb_scr[...] = x_ref[...].astype(BF16)
        acc[...] = jnp.zeros_like(acc)
        row_iota = lax.broadcasted_iota(jnp.int32, (n_keys, g_ref.shape[1]), 0).astype(F32)

        def group_body(tg, carry):
            base = pl.multiple_of(tg * _GATE_GROUP, _GATE_GROUP)
            per_token = []
            for tl in range(_GATE_GROUP):
                row = pl.ds(base + tl, 1)
                ct = jnp.where(c_ref[row, :] == row_iota, g_ref[row, :], 0.0).astype(BF16)
                jt = jnp.where(j_ref[row, :] == row_iota, 1.0, 0.0).astype(BF16)
                per_token.append(lax.dot_general(ct, jt, _NT, preferred_element_type=F32).astype(BF16))
            gate_scr[:, pl.ds(base, _GATE_GROUP), :] = jnp.transpose(jnp.stack(per_token), (1, 0, 2))
            return carry

        lax.fori_loop(0, tm // _GATE_GROUP, group_body, 0, unroll=4)

    a = jnp.dot(xb_scr[...], ut_ref[...], preferred_element_type=F32)
    gate = jnp.concatenate([gate_scr[kstep * chunks_c + i] for i in range(chunks_c)], axis=-1)
    w = _gelu_exact(a).astype(BF16) * gate
    acc[...] += jnp.dot(w, v_ref[...], preferred_element_type=F32)

    @pl.when(kstep == pl.num_programs(1) - 1)
    def _():
        o_ref[...] = _layernorm(alpha * x_ref[...] + acc[...], ln_g_ref[...], ln_b_ref[...])


def _peer_experts(x1, gates, cidx, jidx, ut_bf16, v_bf16, ln_g, ln_b, *, alpha, n_keys, tm, ec):
    t, d_model = x1.shape
    n_exp = v_bf16.shape[0]
    hk = gates.shape[1]
    n_chunks = n_exp // ec
    return pl.pallas_call(
        functools.partial(_experts_kernel, alpha, n_keys),
        grid=(t // tm, n_chunks),
        in_specs=[
            pl.BlockSpec((tm, d_model), lambda i, k: (i, 0)),
            pl.BlockSpec((tm, hk), lambda i, k: (i, 0)),
            pl.BlockSpec((tm, hk), lambda i, k: (i, 0)),
            pl.BlockSpec((tm, hk), lambda i, k: (i, 0)),
            pl.BlockSpec((d_model, ec), lambda i, k: (0, k)),
            pl.BlockSpec((ec, d_model), lambda i, k: (k, 0)),
            pl.BlockSpec((1, d_model), lambda i, k: (0, 0)),
            pl.BlockSpec((1, d_model), lambda i, k: (0, 0)),
        ],
        out_specs=pl.BlockSpec((tm, d_model), lambda i, k: (i, 0)),
        out_shape=jax.ShapeDtypeStruct((t, d_model), F32),
        scratch_shapes=[
            pltpu.VMEM((n_keys, tm, n_keys), BF16),
            pltpu.VMEM((tm, d_model), F32),
            pltpu.VMEM((tm, d_model), BF16),
        ],
        compiler_params=pltpu.CompilerParams(
            dimension_semantics=("arbitrary", "arbitrary"),
            vmem_limit_bytes=V7X_VMEM_LIMIT_BYTES),
        name="peer_experts",
    )(x1, gates, cidx, jidx, ut_bf16, v_bf16, ln_g, ln_b)


def _largest_tile(n, cap, mult):
    best = None
    for c in range(mult, min(n, cap) + 1, mult):
        if n % c == 0:
            best = c
    assert best is not None, (n, cap, mult)
    return best


def _block_diag(w):
    h, a, b = w.shape
    return jnp.einsum("hij,hg->higj", w, jnp.eye(h, dtype=w.dtype)).reshape(h * a, h * b)


def kernel(x_prompt, x_sample, mem_prompt, cache_mem_k, cache_mem_v, state_lru_conv, state_lru_h, state_sc_conv, w_in, lru_conv_w, lru_conv_b, w_rg_a, b_rg_a, w_rg_x, b_rg_x, lru_lambda, sc_conv_w, w_mem_kv, g_lru, g_sc, g_mem, w_out, ln1_g, ln1_b, peer_wq, peer_keys, peer_u, peer_v, ln2_g, ln2_b):
    depth = w_in.shape[0]
    alpha = (2.0 * depth) ** 0.25
    bsz, seq, d_model = x_prompt.shape
    dbsz, dseq, _ = x_sample.shape
    n_mem, mem_heads, mem_hd = cache_mem_k.shape[2:]
    d_mem = mem_heads * mem_hd
    d_lru = state_lru_h.shape[-1]
    d_sc = state_sc_conv.shape[-1]
    n_heads, _, n_keys, dk2 = peer_keys.shape[1:]
    assert n_keys == V7X_LANES and dk2 == V7X_LANES

    xp, xs = x_prompt, x_sample
    outs = {k: [] for k in ("mk", "mv", "lc_p", "lh_p", "sc_p", "lc_s", "lh_s", "sc_s")}
    row = lambda v: v.reshape(1, -1)
    for l in range(depth):
        wts = dict(
            w_in=w_in[l].astype(BF16), lru_conv_w=lru_conv_w[l], lru_conv_b=row(lru_conv_b[l]),
            wa=_block_diag(w_rg_a[l]).astype(BF16), ba=row(b_rg_a[l]),
            wx=_block_diag(w_rg_x[l]).astype(BF16), bx=row(b_rg_x[l]), lam=row(lru_lambda[l]),
            sc_conv_w=sc_conv_w[l], g_lru=row(g_lru[l]), g_sc=row(g_sc[l]), g_mem=row(g_mem[l]),
            w_out=w_out[l].astype(BF16), ln_g=row(ln1_g[l]), ln_b=row(ln1_b[l]))
        mk, mv = _memkv(mem_prompt, w_mem_kv[l].astype(BF16))
        mix = functools.partial(_mixer, wts=wts, alpha=alpha, n_heads_mem=mem_heads)
        x1p, lc_p, lh_p, sc_p = mix(
            xp, mk, mv,
            jnp.zeros((bsz,) + state_lru_conv.shape[2:], F32), jnp.zeros((bsz, d_lru), F32),
            jnp.zeros((bsz,) + state_sc_conv.shape[2:], F32),
            nb=bsz, ts=_largest_tile(seq, 128, V7X_SUBLANES))
        x1s, lc_s, lh_s, sc_s = mix(
            xs, cache_mem_k[l].reshape(dbsz, n_mem, d_mem), cache_mem_v[l].reshape(dbsz, n_mem, d_mem),
            state_lru_conv[l], state_lru_h[l], state_sc_conv[l],
            nb=_largest_tile(dbsz, 16, V7X_SUBLANES), ts=dseq)

        wq_perm = peer_wq[l].astype(BF16).reshape(d_model, n_heads, 2, dk2).transpose(0, 2, 1, 3)
        wq_perm = wq_perm.reshape(d_model, 2 * n_heads * dk2)
        keys_t = peer_keys[l].astype(BF16).transpose(1, 0, 2, 3)
        ut_bf16 = peer_u[l].T.astype(BF16)
        v_bf16 = peer_v[l].astype(BF16)

        def peer(x1_3d):
            x1 = x1_3d.reshape(-1, d_model)
            t = x1.shape[0]
            gates, cidx, jidx = _peer_topk(x1, wq_perm, keys_t, n_heads=n_heads,
                                           tm=_largest_tile(t, 512, 2 * V7X_LANES))
            x2 = _peer_experts(x1, gates, cidx, jidx, ut_bf16, v_bf16, row(ln2_g[l]), row(ln2_b[l]),
                               alpha=alpha, n_keys=n_keys, tm=_largest_tile(t, 512, _GATE_GROUP),
                               ec=16 * n_keys)
            return x2.reshape(x1_3d.shape)

        xp = peer(x1p)
        xs = peer(x1s)

        outs["mk"].append(mk.reshape(bsz, n_mem, mem_heads, mem_hd))
        outs["mv"].append(mv.reshape(bsz, n_mem, mem_heads, mem_hd))
        for name, val in (("lc_p", lc_p), ("lh_p", lh_p), ("sc_p", sc_p),
                          ("lc_s", lc_s), ("lh_s", lh_s), ("sc_s", sc_s)):
            outs[name].append(val)
    st = {k: jnp.stack(v) for k, v in outs.items()}
    return (xp, xs, st["mk"], st["mv"], st["lc_p"], st["lh_p"], st["sc_p"],
            st["lc_s"], st["lh_s"], st["sc_s"])
```

```python
import functools
import math

import jax
import jax.numpy as jnp
from jax import lax
from jax.experimental import pallas as pl
from jax.experimental.pallas import tpu as pltpu

F32 = jnp.float32
BF16 = jnp.bfloat16

LRU_C = 8.0
LN_EPS = 1e-5
RMS_EPS = 1e-6
PEER_TOPK = 16

V7X_LANES = 128
V7X_SUBLANES = 8
V7X_VMEM_LIMIT_BYTES = 56 * 1024 * 1024

_NT = (((1,), (1,)), ((), ()))


def _gelu_exact(x):
    return 0.5 * x * (1.0 + lax.erf(x * (1.0 / math.sqrt(2.0))))


def _layernorm(x, g, b):
    mu = jnp.mean(x, axis=-1, keepdims=True)
    xc = x - mu
    var = jnp.mean(xc * xc, axis=-1, keepdims=True)
    return xc * lax.rsqrt(var + LN_EPS) * g + b


def _rmsnorm(x, g):
    return x * lax.rsqrt(jnp.mean(x * x, axis=-1, keepdims=True) + RMS_EPS) * g


def _memkv_kernel(mem_ref, w_ref, k_ref, v_ref):
    d_mem = k_ref.shape[-1]
    kv = jnp.dot(mem_ref[0].astype(BF16), w_ref[...], preferred_element_type=F32)
    k_ref[0] = kv[:, :d_mem]
    v_ref[0] = kv[:, d_mem:]


def _memkv(mem, w_kv_bf16):
    bsz, n_mem, d_model = mem.shape
    d_mem = w_kv_bf16.shape[1] // 2
    out = jax.ShapeDtypeStruct((bsz, n_mem, d_mem), F32)
    return pl.pallas_call(
        _memkv_kernel,
        grid=(bsz,),
        in_specs=[
            pl.BlockSpec((1, n_mem, d_model), lambda b: (b, 0, 0)),
            pl.BlockSpec((d_model, 2 * d_mem), lambda b: (0, 0)),
        ],
        out_specs=[
            pl.BlockSpec((1, n_mem, d_mem), lambda b: (b, 0, 0)),
            pl.BlockSpec((1, n_mem, d_mem), lambda b: (b, 0, 0)),
        ],
        out_shape=[out, out],
        name="memkv",
    )(mem, w_kv_bf16)


_CONV_PAD = V7X_SUBLANES


def _causal_conv(buf_ref, x3, w_ref, width):
    nb, ts, c = x3.shape
    buf_ref[:, _CONV_PAD:_CONV_PAD + ts, :] = x3
    acc = x3 * w_ref[width - 1:width, :].reshape(1, 1, c)
    for k in range(width - 1):
        start = _CONV_PAD - (width - 1) + k
        acc = acc + buf_ref[:, start:start + ts, :] * w_ref[k:k + 1, :].reshape(1, 1, c)
    tail = buf_ref[:, _CONV_PAD + ts - (width - 1):_CONV_PAD + ts, :]
    buf_ref[:, _CONV_PAD - (width - 1):_CONV_PAD, :] = tail
    return acc, tail


def _mixer_kernel(alpha, n_heads_mem,
                  x_ref, mk_ref, mv_ref, lc0_ref, h0_ref, sc0_ref,
                  w_in_ref, cw_ref, cb_ref, wa_ref, ba_ref, wx_ref, bx_ref, lam_ref,
                  scw_ref, g_lru_ref, g_sc_ref, g_mem_ref, w_out_ref, ln_g_ref, ln_b_ref,
                  y_ref, lc_ref, h_ref, sc_ref,
                  cbuf_a, cbuf_b, a_scr, b_scr, hall, hcar, q_scr, ymem, s_scr=None, p_scr=None):
    nb, ts, d_model = x_ref.shape
    rows = nb * ts
    d_lru = h_ref.shape[-1]
    d_sc = sc_ref.shape[-1]
    d_mem = mk_ref.shape[-1]
    lru_w = cw_ref.shape[0]
    sc_w = scw_ref.shape[0]
    j = pl.program_id(1)

    @pl.when(j == 0)
    def _():
        cbuf_a[:, _CONV_PAD - (lru_w - 1):_CONV_PAD, :] = lc0_ref[...]
        cbuf_b[:, _CONV_PAD - (sc_w - 1):_CONV_PAD, :] = sc0_ref[...]
        hcar[...] = h0_ref[...]

    x2 = x_ref[...].reshape(rows, d_model)
    proj = jnp.dot(x2.astype(BF16), w_in_ref[...], preferred_element_type=F32)
    o = 0
    xl = proj[:, o:o + d_lru]; o += d_lru
    gate = proj[:, o:o + d_lru]; o += d_lru
    sc_b = proj[:, o:o + d_sc]; o += d_sc
    sc_c = proj[:, o:o + d_sc]; o += d_sc
    sc_x = proj[:, o:o + d_sc]; o += d_sc
    q_scr[...] = proj[:, o:o + d_mem]

    xc3, lc_tail = _causal_conv(cbuf_a, xl.reshape(nb, ts, d_lru), cw_ref, lru_w)
    lc_ref[...] = lc_tail
    xc = xc3.reshape(rows, d_lru) + cb_ref[...]
    xcb = xc.astype(BF16)
    r = jax.nn.sigmoid(jnp.dot(xcb, wa_ref[...], preferred_element_type=F32) + ba_ref[...])
    i = jax.nn.sigmoid(jnp.dot(xcb, wx_ref[...], preferred_element_type=F32) + bx_ref[...])
    lam = lam_ref[...]
    log_sig = -(jnp.maximum(-lam, 0.0) + jnp.log1p(jnp.exp(-jnp.abs(lam))))
    log_a = LRU_C * r * log_sig
    a = jnp.exp(log_a)
    bt = jnp.sqrt(-jnp.tanh(log_a) * (a * a + 1.0)) * (i * xc)
    n_lt = d_lru // V7X_LANES
    a3 = a.reshape(nb, ts, d_lru)
    b3 = bt.reshape(nb, ts, d_lru)
    for c in range(n_lt):
        lt = slice(c * V7X_LANES, (c + 1) * V7X_LANES)
        a_scr[c] = jnp.transpose(a3[:, :, lt], (1, 0, 2))
        b_scr[c] = jnp.transpose(b3[:, :, lt], (1, 0, 2))

    def scan_step(s, hs):
        new = []
        for c in range(n_lt):
            h = a_scr[c, s] * hs[c] + b_scr[c, s]
            hall[c, s] = h
            new.append(h)
        return tuple(new)

    h0 = hcar[...]
    hs = lax.fori_loop(0, ts, scan_step,
                       tuple(h0[:, c * V7X_LANES:(c + 1) * V7X_LANES] for c in range(n_lt)),
                       unroll=8)
    h_last = jnp.concatenate(hs, axis=-1)
    hcar[...] = h_last
    h_ref[...] = h_last
    h_all = jnp.concatenate(
        [jnp.transpose(hall[c], (1, 0, 2)).reshape(rows, V7X_LANES) for c in range(n_lt)], axis=-1)
    y_lru = h_all * _gelu_exact(gate)

    u3, sc_tail = _causal_conv(cbuf_b, (sc_c * sc_x).reshape(nb, ts, d_sc), scw_ref, sc_w)
    sc_ref[...] = sc_tail
    y_sc = sc_b * u3.reshape(rows, d_sc)

    hd = d_mem // n_heads_mem
    head_of_lane = lax.broadcasted_iota(jnp.int32, (ts, d_mem), 1) >> int(math.log2(hd))
    scale = hd ** -0.5

    n_rows_b = n_heads_mem * ts

    def seq_scores(b):
        qb = q_scr[pl.ds(pl.multiple_of(b * ts, ts), ts), :]
        qm = jnp.concatenate(
            [jnp.where(head_of_lane == h, qb, 0.0) for h in range(n_heads_mem)], axis=0)
        return lax.dot_general(qm.astype(BF16), mk_ref[b].astype(BF16), _NT,
                               preferred_element_type=F32) * scale

    def softmax_rows(s):
        p = jnp.exp(s - jnp.max(s, axis=-1, keepdims=True))
        return (p / jnp.sum(p, axis=-1, keepdims=True)).astype(BF16)

    def seq_values(b, pb):
        yh = jnp.dot(pb, mv_ref[b].astype(BF16), preferred_element_type=F32)
        yb = jnp.zeros((ts, d_mem), F32)
        for h in range(n_heads_mem):
            yb = yb + jnp.where(head_of_lane == h, yh[h * ts:(h + 1) * ts, :], 0.0)
        ymem[pl.ds(pl.multiple_of(b * ts, ts), ts), :] = yb

    if s_scr is None:
        def attn_step(b, carry):
            seq_values(b, softmax_rows(seq_scores(b)))
            return carry

        lax.fori_loop(0, nb, attn_step, 0, unroll=2)
    else:
        def score_step(b, carry):
            s_scr[pl.ds(pl.multiple_of(b * n_rows_b, n_rows_b), n_rows_b), :] = seq_scores(b)
            return carry

        lax.fori_loop(0, nb, score_step, 0, unroll=2)
        p_scr[...] = softmax_rows(s_scr[...])

        def value_step(b, carry):
            seq_values(b, p_scr[pl.ds(pl.multiple_of(b * n_rows_b, n_rows_b), n_rows_b), :])
            return carry

        lax.fori_loop(0, nb, value_step, 0, unroll=2)

    y = jnp.concatenate(
        [_rmsnorm(y_lru, g_lru_ref[...]), _rmsnorm(y_sc, g_sc_ref[...]),
         _rmsnorm(ymem[...], g_mem_ref[...])], axis=-1)
    out = jnp.dot(y.astype(BF16), w_out_ref[...], preferred_element_type=F32)
    y_ref[...] = _layernorm(alpha * x2 + out, ln_g_ref[...], ln_b_ref[...]).reshape(nb, ts, d_model)


def _const_spec(shape):
    nd = len(shape)
    return pl.BlockSpec(shape, lambda i, j: (0,) * nd)


def _mixer(x, mem_k, mem_v, lc0, h0, sc0, wts, *, alpha, n_heads_mem, nb, ts):
    bsz, seq, d_model = x.shape
    n_mem, d_mem = mem_k.shape[1:]
    d_lru = h0.shape[-1]
    d_sc = sc0.shape[-1]
    lru_w = lc0.shape[1] + 1
    sc_w = sc0.shape[1] + 1
    rows = nb * ts
    grid = (bsz // nb, seq // ts)
    weights = [wts[k] for k in ("w_in", "lru_conv_w", "lru_conv_b", "wa", "ba", "wx", "bx", "lam",
                                "sc_conv_w", "g_lru", "g_sc", "g_mem", "w_out", "ln_g", "ln_b")]
    in_specs = [
        pl.BlockSpec((nb, ts, d_model), lambda i, j: (i, j, 0)),
        pl.BlockSpec((nb, n_mem, d_mem), lambda i, j: (i, 0, 0)),
        pl.BlockSpec((nb, n_mem, d_mem), lambda i, j: (i, 0, 0)),
        pl.BlockSpec((nb, lru_w - 1, d_lru), lambda i, j: (i, 0, 0)),
        pl.BlockSpec((nb, d_lru), lambda i, j: (i, 0)),
        pl.BlockSpec((nb, sc_w - 1, d_sc), lambda i, j: (i, 0, 0)),
    ] + [_const_spec(w.shape) for w in weights]
    out_specs = [
        pl.BlockSpec((nb, ts, d_model), lambda i, j: (i, j, 0)),
        pl.BlockSpec((nb, lru_w - 1, d_lru), lambda i, j: (i, 0, 0)),
        pl.BlockSpec((nb, d_lru), lambda i, j: (i, 0)),
        pl.BlockSpec((nb, sc_w - 1, d_sc), lambda i, j: (i, 0, 0)),
    ]
    out_shape = [
        jax.ShapeDtypeStruct((bsz, seq, d_model), F32),
        jax.ShapeDtypeStruct((bsz, lru_w - 1, d_lru), F32),
        jax.ShapeDtypeStruct((bsz, d_lru), F32),
        jax.ShapeDtypeStruct((bsz, sc_w - 1, d_sc), F32),
    ]
    scratch = [
        pltpu.VMEM((nb, ts + _CONV_PAD, d_lru), F32),
        pltpu.VMEM((nb, ts + _CONV_PAD, d_sc), F32),
        pltpu.VMEM((d_lru // V7X_LANES, ts, nb, V7X_LANES), F32),
        pltpu.VMEM((d_lru // V7X_LANES, ts, nb, V7X_LANES), F32),
        pltpu.VMEM((d_lru // V7X_LANES, ts, nb, V7X_LANES), F32),
        pltpu.VMEM((nb, d_lru), F32),
        pltpu.VMEM((rows, d_mem), F32),
        pltpu.VMEM((rows, d_mem), F32),
    ]
    if ts < V7X_LANES:
        scratch += [pltpu.VMEM((rows * n_heads_mem, n_mem), F32),
                    pltpu.VMEM((rows * n_heads_mem, n_mem), BF16)]
    return pl.pallas_call(
        functools.partial(_mixer_kernel, alpha, n_heads_mem),
        grid=grid,
        in_specs=in_specs,
        out_specs=out_specs,
        out_shape=out_shape,
        scratch_shapes=scratch,
        compiler_params=pltpu.CompilerParams(
            dimension_semantics=("arbitrary", "arbitrary"),
            vmem_limit_bytes=V7X_VMEM_LIMIT_BYTES),
        name="mixer",
    )(x, mem_k, mem_v, lc0, h0, sc0, *weights)


def _argmax_tree(nodes):
    nodes = list(nodes)
    while len(nodes) > 1:
        nxt = []
        for i in range(0, len(nodes) - 1, 2):
            (av, ak), (bv, bk) = nodes[i], nodes[i + 1]
            nxt.append((jnp.maximum(av, bv), jnp.where(av >= bv, ak, bk)))
        if len(nodes) % 2:
            nxt.append(nodes[-1])
        nodes = nxt
    return nodes[0]


def _pair_list():
    k = PEER_TOPK
    return [(a, b) for a in range(k) for b in range(k // (a + 1))]


def _sort_network(n):
    pairs = []

    def merge(lo, hi, r):
        step = r * 2
        if step < hi - lo:
            merge(lo, hi, step)
            merge(lo + r, hi, step)
            pairs.extend((i, i + r) for i in range(lo + r, hi - r, step))
        else:
            pairs.append((lo, lo + r))

    def sort(lo, hi):
        if hi - lo >= 1:
            mid = lo + (hi - lo) // 2
            sort(lo, mid)
            sort(mid + 1, hi)
            merge(lo, hi, 1)

    sort(0, n - 1)
    return tuple(pairs)


def _compare_exchange(v, k, i, j):
    ge = v[i] >= v[j]
    v[i], v[j] = jnp.maximum(v[i], v[j]), jnp.minimum(v[i], v[j])
    k[i], k[j] = jnp.where(ge, k[i], k[j]), jnp.where(ge, k[j], k[i])


def _sorted_top16(rows):
    k = PEER_TOPK
    net = _sort_network(k)
    pad = -len(rows) % k
    rows = list(rows) + [(jnp.full_like(rows[0][0], -jnp.inf), float("inf"))] * pad
    groups = []
    for g0 in range(0, len(rows), k):
        v = [r[0] for r in rows[g0:g0 + k]]
        kk = [r[1] for r in rows[g0:g0 + k]]
        for i, j in net:
            _compare_exchange(v, kk, i, j)
        groups.append((v, kk))
    tv, tk = groups[0]
    dropped = None
    for bv, bk in groups[1:]:
        cv, ck = [], []
        for i in range(k):
            a, b = tv[i], bv[k - 1 - i]
            cv.append(jnp.maximum(a, b))
            ck.append(jnp.where(a >= b, tk[i], bk[k - 1 - i]))
            lo = jnp.minimum(a, b)
            dropped = lo if dropped is None else jnp.maximum(dropped, lo)
        d = k // 2
        while d >= 1:
            for i in range(k):
                if not i & d:
                    _compare_exchange(cv, ck, i, i + d)
            d //= 2
        tv, tk = cv, ck
    for start in (0, 1):
        for i in range(start, k - 1, 2):
            swap = (tv[i] == tv[i + 1]) & (tk[i] > tk[i + 1])
            tk[i], tk[i + 1] = jnp.where(swap, tk[i + 1], tk[i]), jnp.where(swap, tk[i], tk[i + 1])
    suspect = jnp.where(tv[k - 1] <= dropped, 1.0, 0.0)
    for i in range(k - 2):
        suspect = jnp.maximum(suspect, jnp.where(tv[i] == tv[i + 2], 1.0, 0.0))
    return tv, tk, suspect


def _topk_kernel(n_keys, n_heads, x_ref, wq_ref, kb_ref, g_ref, c_ref, j_ref,
                 q_scr, sa_scr, sb_scr, v_scr, i_scr, cv_scr, ck_scr, ts_scr, tk_scr):
    tm = x_ref.shape[0]
    lanes = V7X_LANES
    k = PEER_TOPK
    n_experts = n_keys * n_keys
    dk2 = kb_ref.shape[3]
    n_blk = tm // lanes
    pairs = _pair_list()
    q_scr[...] = jnp.dot(x_ref[...].astype(BF16), wq_ref[...],
                         preferred_element_type=F32).astype(BF16)
    none = jnp.full((n_heads, lanes), -1.0, F32)

    def write_scores(blk, s_scr):
        q = q_scr[pl.ds(pl.multiple_of(blk * lanes, lanes), lanes), :]
        for p in range(2):
            per_head = []
            for h in range(n_heads):
                col = (p * n_heads + h) * dk2
                per_head.append(lax.dot_general(kb_ref[p, h], q[:, col:col + dk2], _NT,
                                                preferred_element_type=F32))
            s_scr[p] = jnp.transpose(jnp.stack(per_head), (1, 0, 2))

    def select(blk, s_scr, next_blk, next_scr):
        rows = pl.ds(pl.multiple_of(blk * lanes, lanes), lanes)
        write_scores(next_blk, next_scr)

        suspect = None
        for p in range(2):
            tv, tk, sus = _sorted_top16([(s_scr[p, key], float(key)) for key in range(n_keys)])
            for r in range(k):
                v_scr[p, r] = tv[r]
                i_scr[p, r] = tk[r]
            suspect = sus if suspect is None else jnp.maximum(suspect, sus)

        @pl.when(jnp.max(suspect) > 0.0)
        def _():
            def rank_body(it, prev):
                new = []
                for p in range(2):
                    leaves = []
                    for key in range(n_keys):
                        v = jnp.where(prev[p] == float(key), -jnp.inf, s_scr[p, key])
                        s_scr[p, key] = v
                        leaves.append((v, float(key)))
                    m, idx = _argmax_tree(leaves)
                    v_scr[p, it] = m
                    i_scr[p, it] = idx
                    new.append(idx)
                return tuple(new)

            lax.fori_loop(0, k, rank_body, (none, none))

        for n, (a, b) in enumerate(pairs):
            cv_scr[n] = v_scr[0, a] + v_scr[1, b]
            ck_scr[n] = float((a * k + b) * n_experts) + (i_scr[0, a] * float(n_keys) + i_scr[1, b])
        tv, tk, sus = _sorted_top16([(cv_scr[n], ck_scr[n]) for n in range(len(pairs))])
        for r in range(k):
            ts_scr[r] = tv[r]
            tk_scr[r] = tk[r]

        @pl.when(jnp.max(sus) > 0.0)
        def _():
            def pair_body(it, prev):
                leaves = []
                for n in range(len(pairs)):
                    kc = ck_scr[n]
                    v = jnp.where(kc == prev, -jnp.inf, cv_scr[n])
                    cv_scr[n] = v
                    leaves.append((v, kc))
                m, kk = _argmax_tree(leaves)
                ts_scr[it] = m
                tk_scr[it] = kk
                return kk

            lax.fori_loop(0, k, pair_body, none)

        top = ts_scr[...]
        ex = jnp.exp(top - top[0:1])
        gates = ex / jnp.sum(ex, axis=0, keepdims=True)
        e = tk_scr[...].astype(jnp.int32) & (n_experts - 1)
        g_ref[rows, :] = gates.reshape(k * n_heads, lanes).T
        c_ref[rows, :] = (e >> int(math.log2(n_keys))).astype(F32).reshape(k * n_heads, lanes).T
        j_ref[rows, :] = (e & (n_keys - 1)).astype(F32).reshape(k * n_heads, lanes).T

    write_scores(0, sa_scr)

    def pair_of_blocks(i, carry):
        blk = 2 * i
        select(blk, sa_scr, blk + 1, sb_scr)
        select(blk + 1, sb_scr, jnp.minimum(blk + 2, n_blk - 1), sa_scr)
        return carry

    lax.fori_loop(0, n_blk // 2, pair_of_blocks, 0)


def _peer_topk(x1, wq_perm, keys_t, *, n_heads, tm):
    t, d_model = x1.shape
    n_keys = keys_t.shape[2]
    hk = n_heads * PEER_TOPK
    n_pairs = len(_pair_list())
    out = jax.ShapeDtypeStruct((t, hk), F32)
    vreg = (n_heads, V7X_LANES)
    assert tm % (2 * V7X_LANES) == 0
    return pl.pallas_call(
        functools.partial(_topk_kernel, n_keys, n_heads),
        grid=(t // tm,),
        in_specs=[
            pl.BlockSpec((tm, d_model), lambda i: (i, 0)),
            pl.BlockSpec(wq_perm.shape, lambda i: (0, 0)),
            pl.BlockSpec(keys_t.shape, lambda i: (0, 0, 0, 0)),
        ],
        out_specs=[pl.BlockSpec((tm, hk), lambda i: (i, 0))] * 3,
        out_shape=[out, out, out],
        scratch_shapes=[
            pltpu.VMEM((tm, wq_perm.shape[1]), BF16),
            pltpu.VMEM((2, n_keys) + vreg, F32),
            pltpu.VMEM((2, n_keys) + vreg, F32),
            pltpu.VMEM((2, PEER_TOPK) + vreg, F32),
            pltpu.VMEM((2, PEER_TOPK) + vreg, F32),
            pltpu.VMEM((n_pairs,) + vreg, F32),
            pltpu.VMEM((n_pairs,) + vreg, F32),
            pltpu.VMEM((PEER_TOPK,) + vreg, F32),
            pltpu.VMEM((PEER_TOPK,) + vreg, F32),
        ],
        compiler_params=pltpu.CompilerParams(
            dimension_semantics=("arbitrary",),
            vmem_limit_bytes=V7X_VMEM_LIMIT_BYTES),
        name="peer_topk",
    )(x1, wq_perm, keys_t)


_GATE_GROUP = 2 * V7X_SUBLANES


def _experts_kernel(alpha, n_keys, x_ref, g_ref, c_ref, j_ref, ut_ref, v_ref, ln_g_ref, ln_b_ref,
                    o_ref, gate_scr, acc, xb_scr):
    tm, d_model = x_ref.shape
    ec = ut_ref.shape[1]
    chunks_c = ec // n_keys
    kstep = pl.program_id(1)

    @pl.when(kstep == 0)
    def _():
        xb_scr[...] = x_ref[...].astype(BF16)
        acc[...] = jnp.zeros_like(acc)
        row_iota = lax.broadcasted_iota(jnp.int32, (n_keys, g_ref.shape[1]), 0).astype(F32)

        def group_body(tg, carry):
            base = pl.multiple_of(tg * _GATE_GROUP, _GATE_GROUP)
            per_token = []
            for tl in range(_GATE_GROUP):
                row = pl.ds(base + tl, 1)
                ct = jnp.where(c_ref[row, :] == row_iota, g_ref[row, :], 0.0).astype(BF16)
                jt = jnp.where(j_ref[row, :] == row_iota, 1.0, 0.0).T.astype(BF16)
                per_token.append(jnp.dot(ct, jt, preferred_element_type=F32).astype(BF16))
            gate_scr[:, pl.ds(base, _GATE_GROUP), :] = jnp.transpose(jnp.stack(per_token), (1, 0, 2))
            return carry

        lax.fori_loop(0, tm // _GATE_GROUP, group_body, 0, unroll=4)

    a = jnp.dot(xb_scr[...], ut_ref[...], preferred_element_type=F32)
    gate = jnp.concatenate([gate_scr[kstep * chunks_c + i] for i in range(chunks_c)], axis=-1)
    w = _gelu_exact(a).astype(BF16) * gate
    acc[...] += jnp.dot(w, v_ref[...], preferred_element_type=F32)

    @pl.when(kstep == pl.num_programs(1) - 1)
    def _():
        o_ref[...] = _layernorm(alpha * x_ref[...] + acc[...], ln_g_ref[...], ln_b_ref[...])


def _peer_experts(x1, gates, cidx, jidx, ut_bf16, v_bf16, ln_g, ln_b, *, alpha, n_keys, tm, ec):
    t, d_model = x1.shape
    n_exp = v_bf16.shape[0]
    hk = gates.shape[1]
    n_chunks = n_exp // ec
    return pl.pallas_call(
        functools.partial(_experts_kernel, alpha, n_keys),
        grid=(t // tm, n_chunks),
        in_specs=[
            pl.BlockSpec((tm, d_model), lambda i, k: (i, 0)),
            pl.BlockSpec((tm, hk), lambda i, k: (i, 0)),
            pl.BlockSpec((tm, hk), lambda i, k: (i, 0)),
            pl.BlockSpec((tm, hk), lambda i, k: (i, 0)),
            pl.BlockSpec((d_model, ec), lambda i, k: (0, k)),
            pl.BlockSpec((ec, d_model), lambda i, k: (k, 0)),
            pl.BlockSpec((1, d_model), lambda i, k: (0, 0)),
            pl.BlockSpec((1, d_model), lambda i, k: (0, 0)),
        ],
        out_specs=pl.BlockSpec((tm, d_model), lambda i, k: (i, 0)),
        out_shape=jax.ShapeDtypeStruct((t, d_model), F32),
        scratch_shapes=[
            pltpu.VMEM((n_keys, tm, n_keys), BF16),
            pltpu.VMEM((tm, d_model), F32),
            pltpu.VMEM((tm, d_model), BF16),
        ],
        compiler_params=pltpu.CompilerParams(
            dimension_semantics=("arbitrary", "arbitrary"),
            vmem_limit_bytes=V7X_VMEM_LIMIT_BYTES),
        name="peer_experts",
    )(x1, gates, cidx, jidx, ut_bf16, v_bf16, ln_g, ln_b)


def _largest_tile(n, cap, mult):
    best = None
    for c in range(mult, min(n, cap) + 1, mult):
        if n % c == 0:
            best = c
    assert best is not None, (n, cap, mult)
    return best


def _block_diag(w):
    h, a, b = w.shape
    return jnp.einsum("hij,hg->higj", w, jnp.eye(h, dtype=w.dtype)).reshape(h * a, h * b)


def kernel(x_prompt, x_sample, mem_prompt, cache_mem_k, cache_mem_v, state_lru_conv, state_lru_h, state_sc_conv, w_in, lru_conv_w, lru_conv_b, w_rg_a, b_rg_a, w_rg_x, b_rg_x, lru_lambda, sc_conv_w, w_mem_kv, g_lru, g_sc, g_mem, w_out, ln1_g, ln1_b, peer_wq, peer_keys, peer_u, peer_v, ln2_g, ln2_b):
    depth = w_in.shape[0]
    alpha = (2.0 * depth) ** 0.25
    bsz, seq, d_model = x_prompt.shape
    dbsz, dseq, _ = x_sample.shape
    n_mem, mem_heads, mem_hd = cache_mem_k.shape[2:]
    d_mem = mem_heads * mem_hd
    d_lru = state_lru_h.shape[-1]
    d_sc = state_sc_conv.shape[-1]
    n_heads, _, n_keys, dk2 = peer_keys.shape[1:]
    assert n_keys == V7X_LANES and dk2 == V7X_LANES

    xp, xs = x_prompt, x_sample
    outs = {k: [] for k in ("mk", "mv", "lc_p", "lh_p", "sc_p", "lc_s", "lh_s", "sc_s")}
    row = lambda v: v.reshape(1, -1)
    for l in range(depth):
        wts = dict(
            w_in=w_in[l].astype(BF16), lru_conv_w=lru_conv_w[l], lru_conv_b=row(lru_conv_b[l]),
            wa=_block_diag(w_rg_a[l]).astype(BF16), ba=row(b_rg_a[l]),
            wx=_block_diag(w_rg_x[l]).astype(BF16), bx=row(b_rg_x[l]), lam=row(lru_lambda[l]),
            sc_conv_w=sc_conv_w[l], g_lru=row(g_lru[l]), g_sc=row(g_sc[l]), g_mem=row(g_mem[l]),
            w_out=w_out[l].astype(BF16), ln_g=row(ln1_g[l]), ln_b=row(ln1_b[l]))
        mk, mv = _memkv(mem_prompt, w_mem_kv[l].astype(BF16))
        mix = functools.partial(_mixer, wts=wts, alpha=alpha, n_heads_mem=mem_heads)
        x1p, lc_p, lh_p, sc_p = mix(
            xp, mk, mv,
            jnp.zeros((bsz,) + state_lru_conv.shape[2:], F32), jnp.zeros((bsz, d_lru), F32),
            jnp.zeros((bsz,) + state_sc_conv.shape[2:], F32),
            nb=bsz, ts=_largest_tile(seq, 128, V7X_SUBLANES))
        x1s, lc_s, lh_s, sc_s = mix(
            xs, cache_mem_k[l].reshape(dbsz, n_mem, d_mem), cache_mem_v[l].reshape(dbsz, n_mem, d_mem),
            state_lru_conv[l], state_lru_h[l], state_sc_conv[l],
            nb=_largest_tile(dbsz, 16, V7X_SUBLANES), ts=dseq)

        wq_perm = peer_wq[l].astype(BF16).reshape(d_model, n_heads, 2, dk2).transpose(0, 2, 1, 3)
        wq_perm = wq_perm.reshape(d_model, 2 * n_heads * dk2)
        keys_t = peer_keys[l].astype(BF16).transpose(1, 0, 2, 3)
        ut_bf16 = peer_u[l].T.astype(BF16)
        v_bf16 = peer_v[l].astype(BF16)

        def peer(x1_3d):
            x1 = x1_3d.reshape(-1, d_model)
            t = x1.shape[0]
            gates, cidx, jidx = _peer_topk(x1, wq_perm, keys_t, n_heads=n_heads,
                                           tm=_largest_tile(t, 512, 2 * V7X_LANES))
            x2 = _peer_experts(x1, gates, cidx, jidx, ut_bf16, v_bf16, row(ln2_g[l]), row(ln2_b[l]),
                               alpha=alpha, n_keys=n_keys, tm=_largest_tile(t, 512, _GATE_GROUP),
                               ec=16 * n_keys)
            return x2.reshape(x1_3d.shape)

        xp = peer(x1p)
        xs = peer(x1s)

        outs["mk"].append(mk.reshape(bsz, n_mem, mem_heads, mem_hd))
        outs["mv"].append(mv.reshape(bsz, n_mem, mem_heads, mem_hd))
        for name, val in (("lc_p", lc_p), ("lh_p", lh_p), ("sc_p", sc_p),
                          ("lc_s", lc_s), ("lh_s", lh_s), ("sc_s", sc_s)):
            outs[name].append(val)
    st = {k: jnp.stack(v) for k, v in outs.items()}
    return (xp, xs, st["mk"], st["mv"], st["lc_p"], st["lh_p"], st["sc_p"],
            st["lc_s"], st["lh_s"], st["sc_s"])
```

```python
import functools
import math

import jax
import jax.numpy as jnp
from jax import lax
from jax.experimental import pallas as pl
from jax.experimental.pallas import tpu as pltpu

F32 = jnp.float32
BF16 = jnp.bfloat16

LRU_C = 8.0
LN_EPS = 1e-5
RMS_EPS = 1e-6
PEER_TOPK = 16

V7X_LANES = 128
V7X_SUBLANES = 8
V7X_VMEM_LIMIT_BYTES = 56 * 1024 * 1024

_NT = (((1,), (1,)), ((), ()))


def _gelu_exact(x):
    return 0.5 * x * (1.0 + lax.erf(x * (1.0 / math.sqrt(2.0))))


def _layernorm(x, g, b):
    mu = jnp.mean(x, axis=-1, keepdims=True)
    xc = x - mu
    var = jnp.mean(xc * xc, axis=-1, keepdims=True)
    return xc * lax.rsqrt(var + LN_EPS) * g + b


def _rmsnorm(x, g):
    return x * lax.rsqrt(jnp.mean(x * x, axis=-1, keepdims=True) + RMS_EPS) * g


def _memkv_kernel(mem_ref, w_ref, k_ref, v_ref):
    d_mem = k_ref.shape[-1]
    kv = jnp.dot(mem_ref[0].astype(BF16), w_ref[...], preferred_element_type=F32)
    k_ref[0] = kv[:, :d_mem]
    v_ref[0] = kv[:, d_mem:]


def _memkv(mem, w_kv_bf16):
    bsz, n_mem, d_model = mem.shape
    d_mem = w_kv_bf16.shape[1] // 2
    out = jax.ShapeDtypeStruct((bsz, n_mem, d_mem), F32)
    return pl.pallas_call(
        _memkv_kernel,
        grid=(bsz,),
        in_specs=[
            pl.BlockSpec((1, n_mem, d_model), lambda b: (b, 0, 0)),
            pl.BlockSpec((d_model, 2 * d_mem), lambda b: (0, 0)),
        ],
        out_specs=[
            pl.BlockSpec((1, n_mem, d_mem), lambda b: (b, 0, 0)),
            pl.BlockSpec((1, n_mem, d_mem), lambda b: (b, 0, 0)),
        ],
        out_shape=[out, out],
        name="memkv",
    )(mem, w_kv_bf16)


_CONV_PAD = V7X_SUBLANES


def _causal_conv(buf_ref, x3, w_ref, width):
    nb, ts, c = x3.shape
    buf_ref[:, _CONV_PAD:_CONV_PAD + ts, :] = x3
    acc = x3 * w_ref[width - 1:width, :].reshape(1, 1, c)
    for k in range(width - 1):
        start = _CONV_PAD - (width - 1) + k
        acc = acc + buf_ref[:, start:start + ts, :] * w_ref[k:k + 1, :].reshape(1, 1, c)
    tail = buf_ref[:, _CONV_PAD + ts - (width - 1):_CONV_PAD + ts, :]
    buf_ref[:, _CONV_PAD - (width - 1):_CONV_PAD, :] = tail
    return acc, tail


def _mixer_kernel(alpha, n_heads_mem,
                  x_ref, mk_ref, mv_ref, lc0_ref, h0_ref, sc0_ref,
                  w_in_ref, cw_ref, cb_ref, wa_ref, ba_ref, wx_ref, bx_ref, lam_ref,
                  scw_ref, g_lru_ref, g_sc_ref, g_mem_ref, w_out_ref, ln_g_ref, ln_b_ref,
                  y_ref, lc_ref, h_ref, sc_ref,
                  cbuf_a, cbuf_b, a_scr, b_scr, hall, hcar, q_scr, ymem, s_scr=None, p_scr=None):
    nb, ts, d_model = x_ref.shape
    rows = nb * ts
    d_lru = h_ref.shape[-1]
    d_sc = sc_ref.shape[-1]
    d_mem = mk_ref.shape[-1]
    lru_w = cw_ref.shape[0]
    sc_w = scw_ref.shape[0]
    j = pl.program_id(1)

    @pl.when(j == 0)
    def _():
        cbuf_a[:, _CONV_PAD - (lru_w - 1):_CONV_PAD, :] = lc0_ref[...]
        cbuf_b[:, _CONV_PAD - (sc_w - 1):_CONV_PAD, :] = sc0_ref[...]
        hcar[...] = h0_ref[...]

    x2 = x_ref[...].reshape(rows, d_model)
    proj = jnp.dot(x2.astype(BF16), w_in_ref[...], preferred_element_type=F32)
    o = 0
    xl = proj[:, o:o + d_lru]; o += d_lru
    gate = proj[:, o:o + d_lru]; o += d_lru
    sc_b = proj[:, o:o + d_sc]; o += d_sc
    sc_c = proj[:, o:o + d_sc]; o += d_sc
    sc_x = proj[:, o:o + d_sc]; o += d_sc
    q_scr[...] = proj[:, o:o + d_mem]

    xc3, lc_tail = _causal_conv(cbuf_a, xl.reshape(nb, ts, d_lru), cw_ref, lru_w)
    lc_ref[...] = lc_tail
    xc = xc3.reshape(rows, d_lru) + cb_ref[...]
    xcb = xc.astype(BF16)
    r = jax.nn.sigmoid(jnp.dot(xcb, wa_ref[...], preferred_element_type=F32) + ba_ref[...])
    i = jax.nn.sigmoid(jnp.dot(xcb, wx_ref[...], preferred_element_type=F32) + bx_ref[...])
    lam = lam_ref[...]
    log_sig = -(jnp.maximum(-lam, 0.0) + jnp.log1p(jnp.exp(-jnp.abs(lam))))
    log_a = LRU_C * r * log_sig
    a = jnp.exp(log_a)
    bt = jnp.sqrt(-jnp.tanh(log_a) * (a * a + 1.0)) * (i * xc)
    n_lt = d_lru // V7X_LANES
    a3 = a.reshape(nb, ts, d_lru)
    b3 = bt.reshape(nb, ts, d_lru)
    for c in range(n_lt):
        lt = slice(c * V7X_LANES, (c + 1) * V7X_LANES)
        a_scr[c] = jnp.transpose(a3[:, :, lt], (1, 0, 2))
        b_scr[c] = jnp.transpose(b3[:, :, lt], (1, 0, 2))

    def scan_step(s, hs):
        new = []
        for c in range(n_lt):
            h = a_scr[c, s] * hs[c] + b_scr[c, s]
            hall[c, s] = h
            new.append(h)
        return tuple(new)

    h0 = hcar[...]
    hs = lax.fori_loop(0, ts, scan_step,
                       tuple(h0[:, c * V7X_LANES:(c + 1) * V7X_LANES] for c in range(n_lt)),
                       unroll=8)
    h_last = jnp.concatenate(hs, axis=-1)
    hcar[...] = h_last
    h_ref[...] = h_last
    h_all = jnp.concatenate(
        [jnp.transpose(hall[c], (1, 0, 2)).reshape(rows, V7X_LANES) for c in range(n_lt)], axis=-1)
    y_lru = h_all * _gelu_exact(gate)

    u3, sc_tail = _causal_conv(cbuf_b, (sc_c * sc_x).reshape(nb, ts, d_sc), scw_ref, sc_w)
    sc_ref[...] = sc_tail
    y_sc = sc_b * u3.reshape(rows, d_sc)

    hd = d_mem // n_heads_mem
    head_of_lane = lax.broadcasted_iota(jnp.int32, (ts, d_mem), 1) >> int(math.log2(hd))
    scale = hd ** -0.5

    n_rows_b = n_heads_mem * ts

    def seq_scores(b):
        qb = q_scr[pl.ds(pl.multiple_of(b * ts, ts), ts), :]
        qm = jnp.concatenate(
            [jnp.where(head_of_lane == h, qb, 0.0) for h in range(n_heads_mem)], axis=0)
        return lax.dot_general(qm.astype(BF16), mk_ref[b].astype(BF16), _NT,
                               preferred_element_type=F32) * scale

    def softmax_rows(s):
        p = jnp.exp(s - jnp.max(s, axis=-1, keepdims=True))
        return (p / jnp.sum(p, axis=-1, keepdims=True)).astype(BF16)

    def seq_values(b, pb):
        yh = jnp.dot(pb, mv_ref[b].astype(BF16), preferred_element_type=F32)
        yb = jnp.zeros((ts, d_mem), F32)
        for h in range(n_heads_mem):
            yb = yb + jnp.where(head_of_lane == h, yh[h * ts:(h + 1) * ts, :], 0.0)
        ymem[pl.ds(pl.multiple_of(b * ts, ts), ts), :] = yb

    if s_scr is None:
        def attn_step(b, carry):
            seq_values(b, softmax_rows(seq_scores(b)))
            return carry

        lax.fori_loop(0, nb, attn_step, 0, unroll=2)
    else:
        def score_step(b, carry):
            s_scr[pl.ds(pl.multiple_of(b * n_rows_b, n_rows_b), n_rows_b), :] = seq_scores(b)
            return carry

        lax.fori_loop(0, nb, score_step, 0, unroll=2)
        p_scr[...] = softmax_rows(s_scr[...])

        def value_step(b, carry):
            seq_values(b, p_scr[pl.ds(pl.multiple_of(b * n_rows_b, n_rows_b), n_rows_b), :])
            return carry

        lax.fori_loop(0, nb, value_step, 0, unroll=2)

    y = jnp.concatenate(
        [_rmsnorm(y_lru, g_lru_ref[...]), _rmsnorm(y_sc, g_sc_ref[...]),
         _rmsnorm(ymem[...], g_mem_ref[...])], axis=-1)
    out = jnp.dot(y.astype(BF16), w_out_ref[...], preferred_element_type=F32)
    y_ref[...] = _layernorm(alpha * x2 + out, ln_g_ref[...], ln_b_ref[...]).reshape(nb, ts, d_model)


def _const_spec(shape):
    nd = len(shape)
    return pl.BlockSpec(shape, lambda i, j: (0,) * nd)


def _mixer(x, mem_k, mem_v, lc0, h0, sc0, wts, *, alpha, n_heads_mem, nb, ts):
    bsz, seq, d_model = x.shape
    n_mem, d_mem = mem_k.shape[1:]
    d_lru = h0.shape[-1]
    d_sc = sc0.shape[-1]
    lru_w = lc0.shape[1] + 1
    sc_w = sc0.shape[1] + 1
    rows = nb * ts
    grid = (bsz // nb, seq // ts)
    weights = [wts[k] for k in ("w_in", "lru_conv_w", "lru_conv_b", "wa", "ba", "wx", "bx", "lam",
                                "sc_conv_w", "g_lru", "g_sc", "g_mem", "w_out", "ln_g", "ln_b")]
    in_specs = [
        pl.BlockSpec((nb, ts, d_model), lambda i, j: (i, j, 0)),
        pl.BlockSpec((nb, n_mem, d_mem), lambda i, j: (i, 0, 0)),
        pl.BlockSpec((nb, n_mem, d_mem), lambda i, j: (i, 0, 0)),
        pl.BlockSpec((nb, lru_w - 1, d_lru), lambda i, j: (i, 0, 0)),
        pl.BlockSpec((nb, d_lru), lambda i, j: (i, 0)),
        pl.BlockSpec((nb, sc_w - 1, d_sc), lambda i, j: (i, 0, 0)),
    ] + [_const_spec(w.shape) for w in weights]
    out_specs = [
        pl.BlockSpec((nb, ts, d_model), lambda i, j: (i, j, 0)),
        pl.BlockSpec((nb, lru_w - 1, d_lru), lambda i, j: (i, 0, 0)),
        pl.BlockSpec((nb, d_lru), lambda i, j: (i, 0)),
        pl.BlockSpec((nb, sc_w - 1, d_sc), lambda i, j: (i, 0, 0)),
    ]
    out_shape = [
        jax.ShapeDtypeStruct((bsz, seq, d_model), F32),
        jax.ShapeDtypeStruct((bsz, lru_w - 1, d_lru), F32),
        jax.ShapeDtypeStruct((bsz, d_lru), F32),
        jax.ShapeDtypeStruct((bsz, sc_w - 1, d_sc), F32),
    ]
    scratch = [
        pltpu.VMEM((nb, ts + _CONV_PAD, d_lru), F32),
        pltpu.VMEM((nb, ts + _CONV_PAD, d_sc), F32),
        pltpu.VMEM((d_lru // V7X_LANES, ts, nb, V7X_LANES), F32),
        pltpu.VMEM((d_lru // V7X_LANES, ts, nb, V7X_LANES), F32),
        pltpu.VMEM((d_lru // V7X_LANES, ts, nb, V7X_LANES), F32),
        pltpu.VMEM((nb, d_lru), F32),
        pltpu.VMEM((rows, d_mem), F32),
        pltpu.VMEM((rows, d_mem), F32),
    ]
    if ts < V7X_LANES:
        scratch += [pltpu.VMEM((rows * n_heads_mem, n_mem), F32),
                    pltpu.VMEM((rows * n_heads_mem, n_mem), BF16)]
    return pl.pallas_call(
        functools.partial(_mixer_kernel, alpha, n_heads_mem),
        grid=grid,
        in_specs=in_specs,
        out_specs=out_specs,
        out_shape=out_shape,
        scratch_shapes=scratch,
        compiler_params=pltpu.CompilerParams(
            dimension_semantics=("arbitrary", "arbitrary"),
            vmem_limit_bytes=V7X_VMEM_LIMIT_BYTES),
        name="mixer",
    )(x, mem_k, mem_v, lc0, h0, sc0, *weights)


def _argmax_tree(nodes):
    nodes = list(nodes)
    while len(nodes) > 1:
        nxt = []
        for i in range(0, len(nodes) - 1, 2):
            (av, ak), (bv, bk) = nodes[i], nodes[i + 1]
            nxt.append((jnp.maximum(av, bv), jnp.where(av >= bv, ak, bk)))
        if len(nodes) % 2:
            nxt.append(nodes[-1])
        nodes = nxt
    return nodes[0]


def _pair_list():
    k = PEER_TOPK
    return [(a, b) for a in range(k) for b in range(k // (a + 1))]


def _sort_network(n):
    pairs = []

    def merge(lo, hi, r):
        step = r * 2
        if step < hi - lo:
            merge(lo, hi, step)
            merge(lo + r, hi, step)
            pairs.extend((i, i + r) for i in range(lo + r, hi - r, step))
        else:
            pairs.append((lo, lo + r))

    def sort(lo, hi):
        if hi - lo >= 1:
            mid = lo + (hi - lo) // 2
            sort(lo, mid)
            sort(mid + 1, hi)
            merge(lo, hi, 1)

    sort(0, n - 1)
    return tuple(pairs)


def _compare_exchange(v, k, i, j):
    ge = v[i] >= v[j]
    v[i], v[j] = jnp.maximum(v[i], v[j]), jnp.minimum(v[i], v[j])
    k[i], k[j] = jnp.where(ge, k[i], k[j]), jnp.where(ge, k[j], k[i])


def _sorted_top16(rows):
    k = PEER_TOPK
    net = _sort_network(k)
    pad = -len(rows) % k
    rows = list(rows) + [(jnp.full_like(rows[0][0], -jnp.inf), float("inf"))] * pad
    groups = []
    for g0 in range(0, len(rows), k):
        v = [r[0] for r in rows[g0:g0 + k]]
        kk = [r[1] for r in rows[g0:g0 + k]]
        for i, j in net:
            _compare_exchange(v, kk, i, j)
        groups.append((v, kk))
    tv, tk = groups[0]
    dropped = None
    for bv, bk in groups[1:]:
        cv, ck = [], []
        for i in range(k):
            a, b = tv[i], bv[k - 1 - i]
            cv.append(jnp.maximum(a, b))
            ck.append(jnp.where(a >= b, tk[i], bk[k - 1 - i]))
            lo = jnp.minimum(a, b)
            dropped = lo if dropped is None else jnp.maximum(dropped, lo)
        d = k // 2
        while d >= 1:
            for i in range(k):
                if not i & d:
                    _compare_exchange(cv, ck, i, i + d)
            d //= 2
        tv, tk = cv, ck
    for start in (0, 1):
        for i in range(start, k - 1, 2):
            swap = (tv[i] == tv[i + 1]) & (tk[i] > tk[i + 1])
            tk[i], tk[i + 1] = jnp.where(swap, tk[i + 1], tk[i]), jnp.where(swap, tk[i], tk[i + 1])
    suspect = jnp.where(tv[k - 1] <= dropped, 1.0, 0.0)
    for i in range(k - 2):
        suspect = jnp.maximum(suspect, jnp.where(tv[i] == tv[i + 2], 1.0, 0.0))
    return tv, tk, suspect


def _topk_kernel(n_keys, n_heads, x_ref, wq_ref, kb_ref, g_ref, c_ref, j_ref,
                 q_scr, sa_scr, sb_scr, v_scr, i_scr, cv_scr, ck_scr, ts_scr, tk_scr):
    tm = x_ref.shape[0]
    lanes = V7X_LANES
    k = PEER_TOPK
    n_experts = n_keys * n_keys
    dk2 = kb_ref.shape[3]
    n_blk = tm // lanes
    pairs = _pair_list()
    q_scr[...] = jnp.dot(x_ref[...].astype(BF16), wq_ref[...],
                         preferred_element_type=F32).astype(BF16)
    none = jnp.full((n_heads, lanes), -1.0, F32)

    def write_scores(blk, s_scr):
        q = q_scr[pl.ds(pl.multiple_of(blk * lanes, lanes), lanes), :]
        for p in range(2):
            per_head = []
            for h in range(n_heads):
                col = (p * n_heads + h) * dk2
                qt = q[:, col:col + dk2].astype(F32).T.astype(BF16)
                per_head.append(jnp.dot(kb_ref[p, h], qt, preferred_element_type=F32))
            s_scr[p] = jnp.transpose(jnp.stack(per_head), (1, 0, 2))

    def select(blk, s_scr, next_blk, next_scr):
        rows = pl.ds(pl.multiple_of(blk * lanes, lanes), lanes)
        write_scores(next_blk, next_scr)

        suspect = None
        for p in range(2):
            tv, tk, sus = _sorted_top16([(s_scr[p, key], float(key)) for key in range(n_keys)])
            for r in range(k):
                v_scr[p, r] = tv[r]
                i_scr[p, r] = tk[r]
            suspect = sus if suspect is None else jnp.maximum(suspect, sus)

        @pl.when(jnp.max(suspect) > 0.0)
        def _():
            def rank_body(it, prev):
                new = []
                for p in range(2):
                    leaves = []
                    for key in range(n_keys):
                        v = jnp.where(prev[p] == float(key), -jnp.inf, s_scr[p, key])
                        s_scr[p, key] = v
                        leaves.append((v, float(key)))
                    m, idx = _argmax_tree(leaves)
                    v_scr[p, it] = m
                    i_scr[p, it] = idx
                    new.append(idx)
                return tuple(new)

            lax.fori_loop(0, k, rank_body, (none, none))

        for n, (a, b) in enumerate(pairs):
            cv_scr[n] = v_scr[0, a] + v_scr[1, b]
            ck_scr[n] = float((a * k + b) * n_experts) + (i_scr[0, a] * float(n_keys) + i_scr[1, b])
        tv, tk, sus = _sorted_top16([(cv_scr[n], ck_scr[n]) for n in range(len(pairs))])
        for r in range(k):
            ts_scr[r] = tv[r]
            tk_scr[r] = tk[r]

        @pl.when(jnp.max(sus) > 0.0)
        def _():
            def pair_body(it, prev):
                leaves = []
                for n in range(len(pairs)):
                    kc = ck_scr[n]
                    v = jnp.where(kc == prev, -jnp.inf, cv_scr[n])
                    cv_scr[n] = v
                    leaves.append((v, kc))
                m, kk = _argmax_tree(leaves)
                ts_scr[it] = m
                tk_scr[it] = kk
                return kk

            lax.fori_loop(0, k, pair_body, none)

        top = ts_scr[...]
        ex = jnp.exp(top - top[0:1])
        gates = ex / jnp.sum(ex, axis=0, keepdims=True)
        e = tk_scr[...].astype(jnp.int32) & (n_experts - 1)
        g_ref[rows, :] = gates.reshape(k * n_heads, lanes).T
        c_ref[rows, :] = (e >> int(math.log2(n_keys))).astype(F32).reshape(k * n_heads, lanes).T
        j_ref[rows, :] = (e & (n_keys - 1)).astype(F32).reshape(k * n_heads, lanes).T

    write_scores(0, sa_scr)

    def pair_of_blocks(i, carry):
        blk = 2 * i
        select(blk, sa_scr, blk + 1, sb_scr)
        select(blk + 1, sb_scr, jnp.minimum(blk + 2, n_blk - 1), sa_scr)
        return carry

    lax.fori_loop(0, n_blk // 2, pair_of_blocks, 0)


def _peer_topk(x1, wq_perm, keys_t, *, n_heads, tm):
    t, d_model = x1.shape
    n_keys = keys_t.shape[2]
    hk = n_heads * PEER_TOPK
    n_pairs = len(_pair_list())
    out = jax.ShapeDtypeStruct((t, hk), F32)
    vreg = (n_heads, V7X_LANES)
    assert tm % (2 * V7X_LANES) == 0
    return pl.pallas_call(
        functools.partial(_topk_kernel, n_keys, n_heads),
        grid=(t // tm,),
        in_specs=[
            pl.BlockSpec((tm, d_model), lambda i: (i, 0)),
            pl.BlockSpec(wq_perm.shape, lambda i: (0, 0)),
            pl.BlockSpec(keys_t.shape, lambda i: (0, 0, 0, 0)),
        ],
        out_specs=[pl.BlockSpec((tm, hk), lambda i: (i, 0))] * 3,
        out_shape=[out, out, out],
        scratch_shapes=[
            pltpu.VMEM((tm, wq_perm.shape[1]), BF16),
            pltpu.VMEM((2, n_keys) + vreg, F32),
            pltpu.VMEM((2, n_keys) + vreg, F32),
            pltpu.VMEM((2, PEER_TOPK) + vreg, F32),
            pltpu.VMEM((2, PEER_TOPK) + vreg, F32),
            pltpu.VMEM((n_pairs,) + vreg, F32),
            pltpu.VMEM((n_pairs,) + vreg, F32),
            pltpu.VMEM((PEER_TOPK,) + vreg, F32),
            pltpu.VMEM((PEER_TOPK,) + vreg, F32),
        ],
        compiler_params=pltpu.CompilerParams(
            dimension_semantics=("arbitrary",),
            vmem_limit_bytes=V7X_VMEM_LIMIT_BYTES),
        name="peer_topk",
    )(x1, wq_perm, keys_t)


_GATE_GROUP = 2 * V7X_SUBLANES


def _experts_kernel(alpha, n_keys, x_ref, g_ref, c_ref, j_ref, ut_ref, v_ref, ln_g_ref, ln_b_ref,
                    o_ref, gate_scr, acc, xb_scr):
    tm, d_model = x_ref.shape
    ec = ut_ref.shape[1]
    chunks_c = ec // n_keys
    kstep = pl.program_id(1)

    @pl.when(kstep == 0)
    def _():
        xb_scr[...] = x_ref[...].astype(BF16)
        acc[...] = jnp.zeros_like(acc)
        row_iota = lax.broadcasted_iota(jnp.int32, (n_keys, g_ref.shape[1]), 0).astype(F32)

        def group_body(tg, carry):
            base = pl.multiple_of(tg * _GATE_GROUP, _GATE_GROUP)
            per_token = []
            for tl in range(_GATE_GROUP):
                row = pl.ds(base + tl, 1)
                ct = jnp.where(c_ref[row, :] == row_iota, g_ref[row, :], 0.0).astype(BF16)
                jt = jnp.where(j_ref[row, :] == row_iota, 1.0, 0.0).T.astype(BF16)
                per_token.append(jnp.dot(ct, jt, preferred_element_type=F32).astype(BF16))
            gate_scr[:, pl.ds(base, _GATE_GROUP), :] = jnp.transpose(jnp.stack(per_token), (1, 0, 2))
            return carry

        lax.fori_loop(0, tm // _GATE_GROUP, group_body, 0, unroll=4)

    a = jnp.dot(xb_scr[...], ut_ref[...], preferred_element_type=F32)
    gate = jnp.concatenate([gate_scr[kstep * chunks_c + i] for i in range(chunks_c)], axis=-1)
    w = _gelu_exact(a).astype(BF16) * gate
    acc[...] += jnp.dot(w, v_ref[...], preferred_element_type=F32)

    @pl.when(kstep == pl.num_programs(1) - 1)
    def _():
        o_ref[...] = _layernorm(alpha * x_ref[...] + acc[...], ln_g_ref[...], ln_b_ref[...])


def _peer_experts(x1, gates, cidx, jidx, ut_bf16, v_bf16, ln_g, ln_b, *, alpha, n_keys, tm, ec):
    t, d_model = x1.shape
    n_exp = v_bf16.shape[0]
    hk = gates.shape[1]
    n_chunks = n_exp // ec
    return pl.pallas_call(
        functools.partial(_experts_kernel, alpha, n_keys),
        grid=(t // tm, n_chunks),
        in_specs=[
            pl.BlockSpec((tm, d_model), lambda i, k: (i, 0)),
            pl.BlockSpec((tm, hk), lambda i, k: (i, 0)),
            pl.BlockSpec((tm, hk), lambda i, k: (i, 0)),
            pl.BlockSpec((tm, hk), lambda i, k: (i, 0)),
            pl.BlockSpec((d_model, ec), lambda i, k: (0, k)),
            pl.BlockSpec((ec, d_model), lambda i, k: (k, 0)),
            pl.BlockSpec((1, d_model), lambda i, k: (0, 0)),
            pl.BlockSpec((1, d_model), lambda i, k: (0, 0)),
        ],
        out_specs=pl.BlockSpec((tm, d_model), lambda i, k: (i, 0)),
        out_shape=jax.ShapeDtypeStruct((t, d_model), F32),
        scratch_shapes=[
            pltpu.VMEM((n_keys, tm, n_keys), BF16),
            pltpu.VMEM((tm, d_model), F32),
            pltpu.VMEM((tm, d_model), BF16),
        ],
        compiler_params=pltpu.CompilerParams(
            dimension_semantics=("arbitrary", "arbitrary"),
            vmem_limit_bytes=V7X_VMEM_LIMIT_BYTES),
        name="peer_experts",
    )(x1, gates, cidx, jidx, ut_bf16, v_bf16, ln_g, ln_b)


def _largest_tile(n, cap, mult):
    best = None
    for c in range(mult, min(n, cap) + 1, mult):
        if n % c == 0:
            best = c
    assert best is not None, (n, cap, mult)
    return best


def _block_diag(w):
    h, a, b = w.shape
    return jnp.einsum("hij,hg->higj", w, jnp.eye(h, dtype=w.dtype)).reshape(h * a, h * b)


def kernel(x_prompt, x_sample, mem_prompt, cache_mem_k, cache_mem_v, state_lru_conv, state_lru_h, state_sc_conv, w_in, lru_conv_w, lru_conv_b, w_rg_a, b_rg_a, w_rg_x, b_rg_x, lru_lambda, sc_conv_w, w_mem_kv, g_lru, g_sc, g_mem, w_out, ln1_g, ln1_b, peer_wq, peer_keys, peer_u, peer_v, ln2_g, ln2_b):
    depth = w_in.shape[0]
    alpha = (2.0 * depth) ** 0.25
    bsz, seq, d_model = x_prompt.shape
    dbsz, dseq, _ = x_sample.shape
    n_mem, mem_heads, mem_hd = cache_mem_k.shape[2:]
    d_mem = mem_heads * mem_hd
    d_lru = state_lru_h.shape[-1]
    d_sc = state_sc_conv.shape[-1]
    n_heads, _, n_keys, dk2 = peer_keys.shape[1:]
    assert n_keys == V7X_LANES and dk2 == V7X_LANES

    xp, xs = x_prompt, x_sample
    outs = {k: [] for k in ("mk", "mv", "lc_p", "lh_p", "sc_p", "lc_s", "lh_s", "sc_s")}
    row = lambda v: v.reshape(1, -1)
    for l in range(depth):
        wts = dict(
            w_in=w_in[l].astype(BF16), lru_conv_w=lru_conv_w[l], lru_conv_b=row(lru_conv_b[l]),
            wa=_block_diag(w_rg_a[l]).astype(BF16), ba=row(b_rg_a[l]),
            wx=_block_diag(w_rg_x[l]).astype(BF16), bx=row(b_rg_x[l]), lam=row(lru_lambda[l]),
            sc_conv_w=sc_conv_w[l], g_lru=row(g_lru[l]), g_sc=row(g_sc[l]), g_mem=row(g_mem[l]),
            w_out=w_out[l].astype(BF16), ln_g=row(ln1_g[l]), ln_b=row(ln1_b[l]))
        mk, mv = _memkv(mem_prompt, w_mem_kv[l].astype(BF16))
        mix = functools.partial(_mixer, wts=wts, alpha=alpha, n_heads_mem=mem_heads)
        x1p, lc_p, lh_p, sc_p = mix(
            xp, mk, mv,
            jnp.zeros((bsz,) + state_lru_conv.shape[2:], F32), jnp.zeros((bsz, d_lru), F32),
            jnp.zeros((bsz,) + state_sc_conv.shape[2:], F32),
            nb=bsz, ts=_largest_tile(seq, 128, V7X_SUBLANES))
        x1s, lc_s, lh_s, sc_s = mix(
            xs, cache_mem_k[l].reshape(dbsz, n_mem, d_mem), cache_mem_v[l].reshape(dbsz, n_mem, d_mem),
            state_lru_conv[l], state_lru_h[l], state_sc_conv[l],
            nb=_largest_tile(dbsz, 16, V7X_SUBLANES), ts=dseq)

        wq_perm = peer_wq[l].astype(BF16).reshape(d_model, n_heads, 2, dk2).transpose(0, 2, 1, 3)
        wq_perm = wq_perm.reshape(d_model, 2 * n_heads * dk2)
        keys_t = peer_keys[l].astype(BF16).transpose(1, 0, 2, 3)
        ut_bf16 = peer_u[l].T.astype(BF16)
        v_bf16 = peer_v[l].astype(BF16)

        def peer(x1_3d):
            x1 = x1_3d.reshape(-1, d_model)
            t = x1.shape[0]
            gates, cidx, jidx = _peer_topk(x1, wq_perm, keys_t, n_heads=n_heads,
                                           tm=_largest_tile(t, 512, 2 * V7X_LANES))
            x2 = _peer_experts(x1, gates, cidx, jidx, ut_bf16, v_bf16, row(ln2_g[l]), row(ln2_b[l]),
                               alpha=alpha, n_keys=n_keys, tm=_largest_tile(t, 512, _GATE_GROUP),
                               ec=16 * n_keys)
            return x2.reshape(x1_3d.shape)

        xp = peer(x1p)
        xs = peer(x1s)

        outs["mk"].append(mk.reshape(bsz, n_mem, mem_heads, mem_hd))
        outs["mv"].append(mv.reshape(bsz, n_mem, mem_heads, mem_hd))
        for name, val in (("lc_p", lc_p), ("lh_p", lh_p), ("sc_p", sc_p),
                          ("lc_s", lc_s), ("lh_s", lh_s), ("sc_s", sc_s)):
            outs[name].append(val)
    st = {k: jnp.stack(v) for k, v in outs.items()}
    return (xp, xs, st["mk"], st["mv"], st["lc_p"], st["lh_p"], st["sc_p"],
            st["lc_s"], st["lh_s"], st["sc_s"])
```
